```python
import math
import jax, jax.numpy as jnp
from jax import lax
import numpy as np

D_MODEL = 1024
BATCH = 8
SEQ = 4096
DEPTH = 1
DEC_BATCH = 16
DEC_SEQ = 4096
PAST_LEN = 128

D_POOL = 512
POOL_WINDOWS = (2, 4, 8, 16)
N_POOL_GROUPS = len(POOL_WINDOWS)
POOL_GROUP = D_POOL // N_POOL_GROUPS
D_HYENA = 512
HYENA_ORDER = 2
SHORT_CONV = 3
FILTER_EMB = 33
FILTER_HIDDEN = 64
DECAY_TARGET = 1e-2
FAST_DECAY_PCT = 0.3
SLOW_DECAY_PCT = 1.5
N_BRANCH = 2
D_IN = D_POOL + (HYENA_ORDER + 1) * D_HYENA + N_BRANCH * D_MODEL
PEER_HEADS = 8
PEER_NKEYS = 128
PEER_EXPERTS = PEER_NKEYS * PEER_NKEYS
PEER_DQ = 256
PEER_DHALF = PEER_DQ // 2
PEER_TOPK = 16
TOKEN_BLOCK = 128
EPS = 1e-6

kernel_name = "pool_hyena_peer_gated_encoder"


def rmsnorm(x, g):
    xf = x.astype(jnp.float32)
    y = xf * lax.rsqrt(jnp.mean(xf * xf, axis=-1, keepdims=True) + EPS)
    return (y * g.astype(jnp.float32)).astype(x.dtype)


def multiscale_pool(a, pool_w, pool_scale):
    B, L, _ = a.shape
    af = a.astype(jnp.float32).reshape(B, L, N_POOL_GROUPS, POOL_GROUP)
    t = jnp.arange(L)
    outs = []
    for g, w in enumerate(POOL_WINDOWS):
        half = w // 2
        xg = af[:, :, g]
        xp = jnp.pad(xg, ((0, 0), (half + 1, half), (0, 0)))
        c = jnp.cumsum(xp, axis=1)
        win = c[:, w:w + L] - c[:, 0:L]
        cnt = (jnp.minimum(t + half, L) - jnp.maximum(t - half, 0)).astype(jnp.float32)
        outs.append(win / cnt[None, :, None] - xg)
    pooled = jnp.stack(outs, axis=2)
    mixed = jnp.einsum('blgc,gcd->blgd', pooled, pool_w.astype(jnp.float32))
    return (mixed.reshape(B, L, D_POOL) * pool_scale.astype(jnp.float32)).astype(a.dtype)


def short_conv(u, w, b):
    L = u.shape[1]
    up = jnp.pad(u, ((0, 0), (1, 1), (0, 0)))
    return up[:, 0:L] * w[0] + up[:, 1:L + 1] * w[1] + up[:, 2:L + 2] * w[2] + b


def hyena_filters(L, w1, b1, f1, w2, b2, f2, w3):
    f32 = jnp.float32
    t = jnp.linspace(0.0, 1.0, L, dtype=f32)[:, None]
    bands = (FILTER_EMB - 1) // 2
    wpos = 2.0 * math.pi * jnp.arange(L, dtype=f32)[:, None] / L
    freqs = jnp.linspace(1e-4, bands - 1, bands, dtype=f32)[None, :]
    z = jnp.concatenate([t, jnp.cos(freqs * wpos), -jnp.sin(freqs * wpos)], axis=-1)
    h = jnp.sin(f1.astype(f32) * (z @ w1.astype(f32) + b1.astype(f32)))
    h = jnp.sin(f2.astype(f32) * (h @ w2.astype(f32) + b2.astype(f32)))
    h = h @ w3.astype(f32)
    max_decay = math.log(DECAY_TARGET) / FAST_DECAY_PCT
    min_decay = math.log(DECAY_TARGET) / SLOW_DECAY_PCT
    deltas = jnp.linspace(min_decay, max_decay, D_HYENA, dtype=f32)
    decay = jnp.exp(-t * jnp.abs(deltas)[None, :])
    h = h.reshape(L, 2, D_HYENA) * decay[:, None, :]
    return h[:, 0], h[:, 1]


def bidirectional_fftconv(z, kf, kb):
    L = z.shape[1]
    k = jnp.concatenate([kf.at[0].add(kb[0]), jnp.zeros((1, D_HYENA), jnp.float32), kb[1:][::-1]], axis=0)
    k_f = jnp.fft.rfft(k, axis=0)
    z_f = jnp.fft.rfft(z, n=2 * L, axis=1)
    return jnp.fft.irfft(z_f * k_f[None], n=2 * L, axis=1)[:, :L]


def hyena_mixer(u, conv_w, conv_b, w1, b1, f1, w2, b2, f2, w3, bias):
    L = u.shape[1]
    uc = short_conv(u, conv_w, conv_b)
    x0, x1, v = jnp.split(uc, 3, axis=-1)
    z = (v * x1).astype(jnp.float32)
    kf, kb = hyena_filters(L, w1, b1, f1, w2, b2, f2, w3)
    y = bidirectional_fftconv(z, kf, kb) + z * bias.astype(jnp.float32)
    return (y * x0.astype(jnp.float32)).astype(u.dtype)


def peer_block(xb, wq, keys, u_tab, v_tab):
    T = xb.shape[0]
    q = (xb @ wq).reshape(T, PEER_HEADS, 2, PEER_DHALF)
    s = jnp.einsum('thpc,hpkc->thpk', q, keys).astype(jnp.float32)
    sv, si = lax.top_k(s, PEER_TOPK)
    cand = (sv[:, :, 0, :, None] + sv[:, :, 1, None, :]).reshape(T, PEER_HEADS, PEER_TOPK * PEER_TOPK)
    cand_idx = (si[:, :, 0, :, None] * PEER_NKEYS + si[:, :, 1, None, :]).reshape(T, PEER_HEADS, PEER_TOPK * PEER_TOPK)
    best, pos = lax.top_k(cand, PEER_TOPK)
    idx = jnp.take_along_axis(cand_idx, pos, axis=-1).reshape(T, PEER_HEADS * PEER_TOPK)
    gate = jax.nn.softmax(best, axis=-1).reshape(T, PEER_HEADS * PEER_TOPK)
    u_g = jnp.take(u_tab, idx, axis=0)
    v_g = jnp.take(v_tab, idx, axis=0)
    act = jax.nn.gelu(jnp.einsum('ted,td->te', u_g, xb).astype(jnp.float32)) * gate
    return jnp.einsum('te,ted->td', act.astype(xb.dtype), v_g)


def peer(x, wq, keys, u_tab, v_tab):
    B, L, D = x.shape
    blocks = x.reshape(B * (L // TOKEN_BLOCK), TOKEN_BLOCK, D)
    out = lax.map(lambda blk: peer_block(blk, wq, keys, u_tab, v_tab), blocks)
    return out.reshape(B, L, D)


def encoder_layer(x, norm1_g, w_in, pool_w, pool_scale, conv_w, conv_b, filt_w1, filt_b1, filt_f1,
                  filt_w2, filt_b2, filt_f2, filt_w3, hyena_bias, w_pool_out, w_hyena_out, w_o,
                  norm2_g, peer_wq, peer_keys, peer_u, peer_v):
    xn = rmsnorm(x, norm1_g)
    proj = xn @ w_in
    a, hy, gates = jnp.split(proj, [D_POOL, D_POOL + (HYENA_ORDER + 1) * D_HYENA], axis=-1)
    gate_a, gate_b = jnp.split(gates, N_BRANCH, axis=-1)
    ya = multiscale_pool(a, pool_w, pool_scale) @ w_pool_out
    yb = hyena_mixer(hy, conv_w, conv_b, filt_w1, filt_b1, filt_f1, filt_w2, filt_b2, filt_f2,
                     filt_w3, hyena_bias) @ w_hyena_out
    mixed = jax.nn.sigmoid(gate_a) * ya + jax.nn.sigmoid(gate_b) * yb
    h = x + mixed @ w_o
    return h + peer(rmsnorm(h, norm2_g), peer_wq, peer_keys, peer_u, peer_v)


def setup_inputs(seed: int = 0) -> dict:
    key = jax.random.key(seed)
    ks = jax.random.split(key, 26)
    f32 = jnp.float32
    nrm = lambda k, shape, scale: jax.random.normal(k, shape, f32) * scale
    L_ = DEPTH
    return {
        "x_prompt": nrm(ks[0], (BATCH, SEQ, D_MODEL), 1.0),
        "x_sample": nrm(ks[1], (DEC_BATCH, DEC_SEQ, D_MODEL), 1.0),
        "norm1_g": 1.0 + nrm(ks[2], (L_, D_MODEL), 0.05),
        "w_in": nrm(ks[3], (L_, D_MODEL, D_IN), D_MODEL ** -0.5),
        "pool_w": nrm(ks[4], (L_, N_POOL_GROUPS, POOL_GROUP, POOL_GROUP), POOL_GROUP ** -0.5),
        "pool_scale": 1.0 + nrm(ks[5], (L_, D_POOL), 0.1),
        "conv_w": nrm(ks[6], (L_, SHORT_CONV, (HYENA_ORDER + 1) * D_HYENA), SHORT_CONV ** -0.5),
        "conv_b": nrm(ks[7], (L_, (HYENA_ORDER + 1) * D_HYENA), 0.01),
        "filt_w1": nrm(ks[8], (L_, FILTER_EMB, FILTER_HIDDEN), FILTER_EMB ** -0.5),
        "filt_b1": nrm(ks[9], (L_, FILTER_HIDDEN), 0.1),
        "filt_f1": 1.0 + nrm(ks[10], (L_, FILTER_HIDDEN), 0.1),
        "filt_w2": nrm(ks[11], (L_, FILTER_HIDDEN, FILTER_HIDDEN), FILTER_HIDDEN ** -0.5),
        "filt_b2": nrm(ks[12], (L_, FILTER_HIDDEN), 0.1),
        "filt_f2": 1.0 + nrm(ks[13], (L_, FILTER_HIDDEN), 0.1),
        "filt_w3": nrm(ks[14], (L_, FILTER_HIDDEN, 2 * D_HYENA), 0.04 * FILTER_HIDDEN ** -0.5),
        "hyena_bias": nrm(ks[15], (L_, D_HYENA), 1.0),
        "w_pool_out": nrm(ks[16], (L_, D_POOL, D_MODEL), D_POOL ** -0.5),
        "w_hyena_out": nrm(ks[17], (L_, D_HYENA, D_MODEL), D_HYENA ** -0.5),
        "w_o": nrm(ks[18], (L_, D_MODEL, D_MODEL), D_MODEL ** -0.5),
        "norm2_g": 1.0 + nrm(ks[19], (L_, D_MODEL), 0.05),
        "peer_wq": nrm(ks[20], (L_, D_MODEL, PEER_HEADS * PEER_DQ), D_MODEL ** -0.5),
        "peer_keys": nrm(ks[21], (L_, PEER_HEADS, 2, PEER_NKEYS, PEER_DHALF), PEER_DHALF ** -0.5),
        "peer_u": nrm(ks[22], (L_, PEER_EXPERTS, D_MODEL), D_MODEL ** -0.5),
        "peer_v": nrm(ks[23], (L_, PEER_EXPERTS, D_MODEL), PEER_TOPK ** -0.5),
        "normf_g": 1.0 + nrm(ks[24], (D_MODEL,), 0.05),
    }


def reference(x_prompt, x_sample, norm1_g, w_in, pool_w, pool_scale, conv_w, conv_b, filt_w1, filt_b1,
              filt_f1, filt_w2, filt_b2, filt_f2, filt_w3, hyena_bias, w_pool_out, w_hyena_out, w_o,
              norm2_g, peer_wq, peer_keys, peer_u, peer_v, normf_g):
    def trunk(x):
        for l in range(DEPTH):
            x = encoder_layer(x, norm1_g[l], w_in[l], pool_w[l], pool_scale[l], conv_w[l], conv_b[l],
                              filt_w1[l], filt_b1[l], filt_f1[l], filt_w2[l], filt_b2[l], filt_f2[l],
                              filt_w3[l], hyena_bias[l], w_pool_out[l], w_hyena_out[l], w_o[l],
                              norm2_g[l], peer_wq[l], peer_keys[l], peer_u[l], peer_v[l])
        return rmsnorm(x, normf_g)

    y_prompt = trunk(x_prompt)
    y_sample = trunk(x_sample)
    return (y_prompt, y_sample)
```

```python
import functools
import math

import jax
import jax.numpy as jnp
import numpy as np
from jax import lax
from jax.experimental import pallas as pl
from jax.experimental.pallas import tpu as pltpu

F32 = jnp.float32
BF16 = jnp.bfloat16
I32 = jnp.int32

EPS = 1e-6
POOL_WINDOWS = (2, 4, 8, 16)
POOL_HALO = 8
PEER_HEADS = 8
PEER_NKEYS = 128
PEER_TOPK = 16
PEER_SLOTS = PEER_HEADS * PEER_TOPK
FILTER_EMB = 33
DECAY_TARGET = 1e-2
FAST_DECAY_PCT = 0.3
SLOW_DECAY_PCT = 1.5
LANES = 128
SUBLANES = 8
HI_MASK = -65536
MIB = 1024 * 1024


def _params(sem, vmem_mib):
    return pltpu.CompilerParams(dimension_semantics=sem, vmem_limit_bytes=vmem_mib * MIB)


def _dot(a, b):
    return jnp.dot(a, b, preferred_element_type=F32)


def _dot_nt(a, b):
    return lax.dot_general(a, b, (((1,), (1,)), ((), ())), preferred_element_type=F32)


def _dot_f32(a, b):
    return jnp.dot(a, b, preferred_element_type=F32, precision=lax.Precision.HIGHEST)


def _rms(x, g):
    return x * lax.rsqrt(jnp.mean(x * x, axis=-1, keepdims=True) + EPS) * g


def _pack_kernel(t_ref, o_ref):
    x = t_ref[...]
    n = x.shape[1] // 2
    lo = pltpu.bitcast(x[:, :n].astype(BF16).astype(F32), I32)
    hi = pltpu.bitcast(x[:, n:].astype(BF16).astype(F32), I32)
    o_ref[...] = hi | lax.shift_right_logical(lo, 16)


def _pack_table(tab):
    e, d = tab.shape
    rows = 512
    packed = pl.pallas_call(
        _pack_kernel,
        grid=(e // rows,),
        in_specs=[pl.BlockSpec((rows, d), lambda i: (i, 0))],
        out_specs=pl.BlockSpec((rows, d // 2), lambda i: (i, 0)),
        out_shape=jax.ShapeDtypeStruct((e, d // 2), I32),
        compiler_params=_params(("parallel",), 32),
        name="pack_table",
    )(tab)
    return packed.reshape(e * (d // 2) // LANES, LANES)


def _inproj_kernel(x_ref, g_ref, w_ref, ahy_ref, sg_ref, *, n_ahy, chunk):
    xn = _rms(x_ref[...], g_ref[...]).astype(BF16)
    for j in range(n_ahy // chunk):
        ahy_ref[:, j * chunk:(j + 1) * chunk] = _dot(xn, w_ref[:, j * chunk:(j + 1) * chunk])
    n_g = w_ref.shape[1] - n_ahy
    for j in range(n_g // chunk):
        g = _dot(xn, w_ref[:, n_ahy + j * chunk:n_ahy + (j + 1) * chunk])
        sg_ref[:, j * chunk:(j + 1) * chunk] = 1.0 / (1.0 + jnp.exp(-g))


def _inproj(x, g, w_bf, n_ahy, tm):
    t, d = x.shape
    n = w_bf.shape[1]
    return pl.pallas_call(
        functools.partial(_inproj_kernel, n_ahy=n_ahy, chunk=min(1024, n_ahy, n - n_ahy)),
        grid=(t // tm,),
        in_specs=[pl.BlockSpec((tm, d), lambda i: (i, 0)),
                  pl.BlockSpec((1, d), lambda i: (0, 0)),
                  pl.BlockSpec((d, n), lambda i: (0, 0))],
        out_specs=[pl.BlockSpec((tm, n_ahy), lambda i: (i, 0)),
                   pl.BlockSpec((tm, n - n_ahy), lambda i: (i, 0))],
        out_shape=[jax.ShapeDtypeStruct((t, n_ahy), F32), jax.ShapeDtypeStruct((t, n - n_ahy), F32)],
        compiler_params=_params(("parallel",), 56),
        name="inproj",
    )(x, g.reshape(1, d), w_bf)


def _local_kernel(ahy_ref, prev_ref, next_ref, sga_ref, pw_ref, ps_ref, cw_ref, cb_ref, wpo_ref,
                  p1_ref, x0_ref, z_ref, zb_ref, *, seq_len, d_pool, d_hy):
    tm = ahy_ref.shape[0]
    ext_rows = tm + 2 * POOL_HALO
    tiles_per_seq = seq_len // tm
    j = pl.program_id(0) % tiles_per_seq
    prev = jnp.where(j == 0, 0.0, prev_ref[...])
    nxt = jnp.where(j == tiles_per_seq - 1, 0.0, next_ref[...])
    pos = j * tm + lax.broadcasted_iota(I32, (tm, 1), 0)

    def ext(lo, hi):
        return jnp.concatenate([prev[:, lo:hi], ahy_ref[:, lo:hi], nxt[:, lo:hi]], axis=0)

    def shifted(v, k):
        return pltpu.roll(v, k % ext_rows, axis=0)

    group = d_pool // len(POOL_WINDOWS)
    mixed = []
    for g, w in enumerate(POOL_WINDOWS):
        half = w // 2
        e = ext(g * group, (g + 1) * group)
        win = e + shifted(e, 1)
        span = 1
        while span < half:
            win = shifted(win, span) + shifted(win, -span)
            span *= 2
        win = win[POOL_HALO:POOL_HALO + tm]
        cnt = (jnp.minimum(pos + half, seq_len) - jnp.maximum(pos - half, 0)).astype(F32)
        pooled = win / cnt - ahy_ref[:, g * group:(g + 1) * group]
        mixed.append(_dot(pooled.astype(BF16), pw_ref[g]))
    pm = jnp.concatenate(mixed, axis=1) * ps_ref[...]
    p1_ref[...] = sga_ref[...] * _dot(pm.astype(BF16), wpo_ref[...])

    d_br = d_hy // 3
    branches = []
    for b in range(3):
        lo, hi = d_pool + b * d_br, d_pool + (b + 1) * d_br
        e = ext(lo, hi)
        c0, c1 = b * d_br, (b + 1) * d_br
        uc = (shifted(e, 1)[POOL_HALO:POOL_HALO + tm] * cw_ref[0:1, c0:c1]
              + ahy_ref[:, lo:hi] * cw_ref[1:2, c0:c1]
              + shifted(e, -1)[POOL_HALO:POOL_HALO + tm] * cw_ref[2:3, c0:c1]
              + cb_ref[:, c0:c1])
        branches.append(uc)
    x0, x1, v = branches
    z = v * x1
    x0_ref[...] = x0
    z_ref[...] = z
    zb_ref[...] = z.astype(BF16)


def _local(ahy, sg, pool_w_bf, pool_scale, conv_w, conv_b, w_pool_out_bf, seq_len, d_pool, d_hy, tm):
    t, n_ahy = ahy.shape
    d_model = w_pool_out_bf.shape[1]
    d_br = d_hy // 3
    hb = tm // POOL_HALO
    n_halo_blocks = t // POOL_HALO
    return pl.pallas_call(
        functools.partial(_local_kernel, seq_len=seq_len, d_pool=d_pool, d_hy=d_hy),
        grid=(t // tm,),
        in_specs=[pl.BlockSpec((tm, n_ahy), lambda i: (i, 0)),
                  pl.BlockSpec((POOL_HALO, n_ahy), lambda i: (jnp.maximum(i * hb - 1, 0), 0)),
                  pl.BlockSpec((POOL_HALO, n_ahy), lambda i: (jnp.minimum((i + 1) * hb, n_halo_blocks - 1), 0)),
                  pl.BlockSpec((tm, d_model), lambda i: (i, 0)),
                  pl.BlockSpec(pool_w_bf.shape, lambda i: (0, 0, 0)),
                  pl.BlockSpec((1, d_pool), lambda i: (0, 0)),
                  pl.BlockSpec(conv_w.shape, lambda i: (0, 0)),
                  pl.BlockSpec((1, d_hy), lambda i: (0, 0)),
                  pl.BlockSpec(w_pool_out_bf.shape, lambda i: (0, 0))],
        out_specs=[pl.BlockSpec((tm, d_model), lambda i: (i, 0)),
                   pl.BlockSpec((tm, d_br), lambda i: (i, 0)),
                   pl.BlockSpec((tm, d_br), lambda i: (i, 0)),
                   pl.BlockSpec((tm, d_br), lambda i: (i, 0))],
        out_shape=[jax.ShapeDtypeStruct((t, d_model), F32),
                   jax.ShapeDtypeStruct((t, d_br), F32),
                   jax.ShapeDtypeStruct((t, d_br), F32),
                   jax.ShapeDtypeStruct((t, d_br), BF16)],
        compiler_params=_params(("parallel",), 56),
        name="local_mix",
    )(ahy, ahy, ahy, sg, pool_w_bf, pool_scale.reshape(1, d_pool), conv_w, conv_b.reshape(1, d_hy), w_pool_out_bf)


def _filter_kernel(freq_ref, w1_ref, b1_ref, f1_ref, w2_ref, b2_ref, f2_ref, w3_ref, dl_ref, ks_ref, kd_ref,
                   *, seq_len):
    tl = ks_ref.shape[0]
    d_h = ks_ref.shape[1]
    i = (pl.program_id(0) * tl + lax.broadcasted_iota(I32, (tl, 1), 0)).astype(F32)
    t = i / (seq_len - 1.0)
    wpos = (2.0 * math.pi) * i / seq_len
    lane = lax.broadcasted_iota(I32, (tl, LANES), 1)
    bands = (FILTER_EMB - 1) // 2
    ang = freq_ref[...] * wpos
    feat = jnp.where(lane == 0, t,
                     jnp.where(lane <= bands, jnp.cos(ang),
                               jnp.where(lane <= 2 * bands, -jnp.sin(ang), 0.0)))
    h = jnp.sin(f1_ref[...] * (_dot_f32(feat, w1_ref[...]) + b1_ref[...]))
    h = jnp.sin(f2_ref[...] * (_dot_f32(h, w2_ref[...]) + b2_ref[...]))
    h = _dot_f32(h, w3_ref[...])
    decay = jnp.exp(-t * jnp.abs(dl_ref[...]))
    kf = h[:, :d_h] * decay
    kb = h[:, d_h:] * decay
    ks_ref[...] = kf + kb
    kd_ref[...] = kf - kb


def _filters(w1, b1, f1, w2, b2, f2, w3, seq_len, d_h):
    hid = w1.shape[1]
    bands = (FILTER_EMB - 1) // 2
    freqs = np.zeros((1, LANES), np.float32)
    fr = np.linspace(1e-4, bands - 1, bands, dtype=np.float32)
    freqs[0, 1:1 + bands] = fr
    freqs[0, 1 + bands:1 + 2 * bands] = fr
    w1p = jnp.zeros((LANES, hid), F32).at[:FILTER_EMB].set(w1)
    max_decay = math.log(DECAY_TARGET) / FAST_DECAY_PCT
    min_decay = math.log(DECAY_TARGET) / SLOW_DECAY_PCT
    deltas = np.linspace(min_decay, max_decay, d_h, dtype=np.float32).reshape(1, d_h)
    tl = min(512, seq_len)
    full = lambda shape: pl.BlockSpec(shape, lambda i: tuple(0 for _ in shape))
    return pl.pallas_call(
        functools.partial(_filter_kernel, seq_len=seq_len),
        grid=(seq_len // tl,),
        in_specs=[full((1, LANES)), full((LANES, hid)), full((1, hid)), full((1, hid)),
                  full((hid, hid)), full((1, hid)), full((1, hid)), full((hid, 2 * d_h)), full((1, d_h))],
        out_specs=[pl.BlockSpec((tl, d_h), lambda i: (i, 0)), pl.BlockSpec((tl, d_h), lambda i: (i, 0))],
        out_shape=[jax.ShapeDtypeStruct((seq_len, d_h), F32), jax.ShapeDtypeStruct((seq_len, d_h), F32)],
        compiler_params=_params(("parallel",), 32),
        name="hyena_filters",
    )(jnp.asarray(freqs), w1p, b1.reshape(1, hid), f1.reshape(1, hid), w2, b2.reshape(1, hid),
      f2.reshape(1, hid), w3, jnp.asarray(deltas))


def _dftgen_kernel(c_ref, s_ref, st_ref, *, n_fft):
    tr, n = c_ref.shape
    row = pl.program_id(0) * tr + lax.broadcasted_iota(I32, (tr, n), 0)
    col = lax.broadcasted_iota(I32, (tr, n), 1)
    ang = ((row * col) & (n_fft - 1)).astype(F32) * (2.0 * math.pi / n_fft)
    c_ref[...] = jnp.cos(ang).astype(BF16)
    ms = -jnp.sin(ang)
    sign_col = (1 - 2 * (col & 1)).astype(F32)
    sign_row = (1 - 2 * (row & 1)).astype(F32)
    s_ref[...] = jnp.where(row == 0, sign_col, ms).astype(BF16)
    st_ref[...] = jnp.where(col == 0, sign_row, ms).astype(BF16)


def _dftgen(seq_len):
    tr = min(256, seq_len)
    spec = pl.BlockSpec((tr, seq_len), lambda i: (i, 0))
    shp = jax.ShapeDtypeStruct((seq_len, seq_len), BF16)
    return pl.pallas_call(
        functools.partial(_dftgen_kernel, n_fft=2 * seq_len),
        grid=(seq_len // tr,),
        out_specs=[spec, spec, spec],
        out_shape=[shp, shp, shp],
        compiler_params=_params(("parallel",), 48),
        name="dft_matrices",
    )()


def _spectrum_kernel(c_ref, s_ref, ks_ref, kd_ref, kr_ref, ki_ref):
    ks = ks_ref[...]
    kr_ref[...] = _dot(c_ref[...], ks.astype(BF16))
    ki = _dot(s_ref[...], kd_ref[...].astype(BF16))
    tf = ki.shape[0]
    row = lax.broadcasted_iota(I32, (tf, 1), 0)
    sign = (1 - 2 * (lax.broadcasted_iota(I32, (ks.shape[0], 1), 0) & 1)).astype(F32)
    nyq = jnp.sum(ks * sign, axis=0, keepdims=True)
    first = jnp.logical_and(pl.program_id(0) == 0, row == 0)
    ki_ref[...] = jnp.where(first, nyq, ki)


def _spectrum(cmat, smat, ks, kd, tf):
    n, d_h = ks.shape
    return pl.pallas_call(
        _spectrum_kernel,
        grid=(n // tf,),
        in_specs=[pl.BlockSpec((tf, n), lambda i: (i, 0)), pl.BlockSpec((tf, n), lambda i: (i, 0)),
                  pl.BlockSpec((n, d_h), lambda i: (0, 0)), pl.BlockSpec((n, d_h), lambda i: (0, 0))],
        out_specs=[pl.BlockSpec((tf, d_h), lambda i: (i, 0)), pl.BlockSpec((tf, d_h), lambda i: (i, 0))],
        out_shape=[jax.ShapeDtypeStruct((n, d_h), F32), jax.ShapeDtypeStruct((n, d_h), F32)],
        compiler_params=_params(("parallel",), 56),
        name="filter_spectrum",
    )(cmat, smat, ks, kd)


def _fwd_kernel(c_ref, s_ref, z_ref, kr_ref, ki_ref, yr_ref, yi_ref, *, n_fft):
    z = z_ref[...]
    zr = _dot(c_ref[...], z)
    zi = _dot(s_ref[...], z)
    kr = kr_ref[...]
    ki = ki_ref[...]
    row = lax.broadcasted_iota(I32, (zr.shape[0], 1), 0)
    first = jnp.logical_and(pl.program_id(0) == 0, row == 0)
    yr = jnp.where(first, zr * kr, zr * kr - zi * ki)
    yi = jnp.where(first, zi * ki, zr * ki + zi * kr)
    scale = jnp.where(first, 1.0 / n_fft, 2.0 / n_fft)
    yr_ref[...] = (yr * scale).astype(BF16)
    yi_ref[...] = (yi * scale).astype(BF16)


def _inv_kernel(c_ref, st_ref, yr_ref, yi_ref, z_ref, x0_ref, bias_ref, o_ref):
    y = _dot(c_ref[...], yr_ref[...]) + _dot(st_ref[...], yi_ref[...])
    o_ref[...] = ((y + z_ref[...] * bias_ref[...]) * x0_ref[...]).astype(BF16)


def _long_conv(cmat, smat, stmat, zb, z, x0, kr, ki, bias, seq_len, tf):
    t, d_h = z.shape
    nb = t // seq_len
    nf = seq_len // tf
    mat_spec = pl.BlockSpec((tf, seq_len), lambda f, b: (f, 0))
    seq_spec = pl.BlockSpec((seq_len, d_h), lambda f, b: (b, 0))
    tile_spec = pl.BlockSpec((tf, d_h), lambda f, b: (b * nf + f, 0))
    k_spec = pl.BlockSpec((tf, d_h), lambda f, b: (f, 0))
    yr, yi = pl.pallas_call(
        functools.partial(_fwd_kernel, n_fft=2 * seq_len),
        grid=(nf, nb),
        in_specs=[mat_spec, mat_spec, seq_spec, k_spec, k_spec],
        out_specs=[tile_spec, tile_spec],
        out_shape=[jax.ShapeDtypeStruct((t, d_h), BF16), jax.ShapeDtypeStruct((t, d_h), BF16)],
        compiler_params=_params(("parallel", "parallel"), 56),
        name="conv_fwd_dft",
    )(cmat, smat, zb, kr, ki)
    return pl.pallas_call(
        _inv_kernel,
        grid=(nf, nb),
        in_specs=[mat_spec, mat_spec, seq_spec, seq_spec, tile_spec, tile_spec,
                  pl.BlockSpec((1, d_h), lambda f, b: (0, 0))],
        out_specs=tile_spec,
        out_shape=jax.ShapeDtypeStruct((t, d_h), BF16),
        compiler_params=_params(("parallel", "parallel"), 56),
        name="conv_inv_dft",
    )(cmat, stmat, yr, yi, z, x0, bias.reshape(1, d_h))


def _top16(s, payload=None):
    n_rows = s.shape[0]
    rid = lax.broadcasted_iota(I32, s.shape, 0).astype(F32)
    vals, ids = [], []
    for _ in range(PEER_TOPK):
        m = jnp.max(s, axis=0, keepdims=True)
        pos = jnp.min(jnp.where(s == m, rid, float(n_rows)), axis=0, keepdims=True)
        hit = rid == pos
        vals.append(m)
        ids.append(pos if payload is None else jnp.max(jnp.where(hit, payload, -1.0), axis=0, keepdims=True))
        s = jnp.where(hit, -jnp.inf, s)
    return jnp.concatenate(vals, axis=0), jnp.concatenate(ids, axis=0)


def _route_kernel(hyb_ref, p1_ref, sgb_ref, x_ref, who_ref, wo_ref, g2_ref, wq_ref, keys_ref,
                  h_ref, hn_ref, idx_ref, gate_ref, q_scr, idx_scr, gate_scr):
    yb = _dot(hyb_ref[...], who_ref[...])
    mixed = p1_ref[...] + sgb_ref[...] * yb
    h = x_ref[...] + _dot(mixed.astype(BF16), wo_ref[...])
    h_ref[...] = h
    hn = _rms(h, g2_ref[...])
    hn_ref[...] = hn
    q = _dot(hn.astype(BF16), wq_ref[...])
    n_half = 2 * PEER_HEADS
    dh = q.shape[1] // n_half
    for j in range(n_half):
        q_scr[j] = q[:, j * dh:(j + 1) * dh].astype(BF16)

    def head(hd, carry):
        sv0, si0 = _top16(_dot_nt(keys_ref[2 * hd], q_scr[2 * hd]))
        sv1, si1 = _top16(_dot_nt(keys_ref[2 * hd + 1], q_scr[2 * hd + 1]))
        cand = jnp.concatenate([sv0[a:a + 1] + sv1 for a in range(PEER_TOPK)], axis=0)
        cidx = jnp.concatenate([si0[a:a + 1] * float(PEER_NKEYS) + si1 for a in range(PEER_TOPK)], axis=0)
        best, eid = _top16(cand, cidx)
        ex = jnp.exp(best - best[0:1])
        r0 = pl.multiple_of(hd * PEER_TOPK, PEER_TOPK)
        idx_scr[pl.ds(r0, PEER_TOPK), :] = eid.astype(I32)
        gate_scr[pl.ds(r0, PEER_TOPK), :] = ex / jnp.sum(ex, axis=0, keepdims=True)
        return carry

    lax.fori_loop(0, PEER_HEADS, head, 0)
    idx_ref[...] = idx_scr[...].T
    gate_ref[...] = gate_scr[...].T


def _route(hyb, p1, sg, x, who_bf, wo_bf, g2, wq_bf, keys_bf, tm):
    t, d = x.shape
    d_h = hyb.shape[1]
    nq = wq_bf.shape[1]
    n_half, nk, dh = keys_bf.shape
    tok = lambda w: pl.BlockSpec((tm, w), lambda i: (i, 0))
    full = lambda a: pl.BlockSpec(a.shape, lambda i: tuple(0 for _ in a.shape))
    return pl.pallas_call(
        _route_kernel,
        grid=(t // tm,),
        in_specs=[tok(d_h), tok(d), pl.BlockSpec((tm, d), lambda i: (i, 1)), tok(d),
                  full(who_bf), full(wo_bf), pl.BlockSpec((1, d), lambda i: (0, 0)), full(wq_bf), full(keys_bf)],
        out_specs=[tok(d), tok(d), tok(PEER_SLOTS), tok(PEER_SLOTS)],
        out_shape=[jax.ShapeDtypeStruct((t, d), F32), jax.ShapeDtypeStruct((t, d), F32),
                   jax.ShapeDtypeStruct((t, PEER_SLOTS), I32), jax.ShapeDtypeStruct((t, PEER_SLOTS), F32)],
        scratch_shapes=[pltpu.VMEM((n_half, tm, dh), BF16),
                        pltpu.VMEM((PEER_SLOTS, tm), I32),
                        pltpu.VMEM((PEER_SLOTS, tm), F32)],
        compiler_params=_params(("parallel",), 56),
        name="route",
    )(hyb, p1, sg, x, who_bf, wo_bf, g2.reshape(1, d), wq_bf, keys_bf)


def _expert_rows(tab_ref, e):
    rows = tab_ref.shape[0] // PEER_NKEYS // PEER_NKEYS
    r = pl.multiple_of(e * rows, rows)
    w = tab_ref[pl.ds(r, rows), :]
    return pltpu.bitcast(w << 16, F32), pltpu.bitcast(w & HI_MASK, F32)


def _gelu(x):
    return 0.5 * x * (1.0 + jnp.tanh(math.sqrt(2.0 / math.pi) * (x + 0.044715 * (x * x * x))))


def _peer_u_kernel(idx_ref, x_ref, gate_ref, tab_ref, act_ref, stage_ref):
    tb = x_ref.shape[0]
    half = x_ref.shape[1] // 2
    ones = jnp.ones((SUBLANES, LANES), BF16)

    def token(t, carry):
        xt = x_ref[t]
        xl, xh = xt[:half], xt[half:]
        for e in range(PEER_SLOTS):
            lo, hi = _expert_rows(tab_ref, idx_ref[t, e])
            stage_ref[e:e + 1, :] = jnp.sum(lo * xl + hi * xh, axis=0, keepdims=True)
        p = stage_ref[...]
        p_hi = p.astype(BF16)
        p_lo = (p - p_hi.astype(F32)).astype(BF16)
        sums = _dot_nt(ones, p_hi) + _dot_nt(ones, p_lo)
        act_ref[pl.ds(t, 1), :] = sums[0:1]
        return carry

    lax.fori_loop(0, tb, token, 0)
    act_ref[...] = _gelu(act_ref[...]) * gate_ref[...]


def _peer_u(idx, x3, gate, tab, tb):
    t = x3.shape[0]
    return pl.pallas_call(
        _peer_u_kernel,
        grid=(t // tb,),
        in_specs=[pl.BlockSpec((tb, PEER_SLOTS), lambda i: (i, 0), memory_space=pltpu.SMEM),
                  pl.BlockSpec((tb,) + x3.shape[1:], lambda i: (i, 0, 0)),
                  pl.BlockSpec((tb, PEER_SLOTS), lambda i: (i, 0)),
                  pl.BlockSpec(tab.shape, lambda i: (0, 0), pipeline_mode=pl.Buffered(1))],
        out_specs=pl.BlockSpec((tb, PEER_SLOTS), lambda i: (i, 0)),
        out_shape=jax.ShapeDtypeStruct((t, PEER_SLOTS), F32),
        scratch_shapes=[pltpu.VMEM((PEER_SLOTS, LANES), F32)],
        compiler_params=_params(("arbitrary",), 48),
        name="peer_u",
    )(idx, x3, gate, tab)


def _peer_v_kernel(idx_ref, act_ref, h_ref, g_ref, tab_ref, y_ref):
    tb = h_ref.shape[0]
    half = h_ref.shape[1] // 2
    n_acc = 4

    def token(t, carry):
        acc_lo = [jnp.zeros((half, LANES), F32) for _ in range(n_acc)]
        acc_hi = [jnp.zeros((half, LANES), F32) for _ in range(n_acc)]
        for e in range(PEER_SLOTS):
            lo, hi = _expert_rows(tab_ref, idx_ref[t, e])
            a = act_ref[t, e]
            acc_lo[e % n_acc] = acc_lo[e % n_acc] + lo * a
            acc_hi[e % n_acc] = acc_hi[e % n_acc] + hi * a
        lo = (acc_lo[0] + acc_lo[1]) + (acc_lo[2] + acc_lo[3])
        hi = (acc_hi[0] + acc_hi[1]) + (acc_hi[2] + acc_hi[3])
        y_ref[t] = jnp.concatenate([lo, hi], axis=0)
        return carry

    lax.fori_loop(0, tb, token, 0)
    y = h_ref[...] + y_ref[...]
    ms = jnp.sum(jnp.sum(y * y, axis=2, keepdims=True), axis=1, keepdims=True) * (1.0 / (y.shape[1] * y.shape[2]))
    y_ref[...] = y * lax.rsqrt(ms + EPS) * g_ref[...]


def _peer_v(idx, act, h3, g3, tab, tb):
    t = h3.shape[0]
    blk = pl.BlockSpec((tb,) + h3.shape[1:], lambda i: (i, 0, 0))
    return pl.pallas_call(
        _peer_v_kernel,
        grid=(t // tb,),
        in_specs=[pl.BlockSpec((tb, PEER_SLOTS), lambda i: (i, 0), memory_space=pltpu.SMEM),
                  pl.BlockSpec((tb, PEER_SLOTS), lambda i: (i, 0), memory_space=pltpu.SMEM),
                  blk,
                  pl.BlockSpec(g3.shape, lambda i: (0, 0, 0)),
                  pl.BlockSpec(tab.shape, lambda i: (0, 0), pipeline_mode=pl.Buffered(1))],
        out_specs=blk,
        out_shape=jax.ShapeDtypeStruct(h3.shape, F32),
        compiler_params=_params(("arbitrary",), 48),
        name="peer_v",
    )(idx, act, h3, g3, tab)


def _pick_tile(n, want):
    while n % want:
        want //= 2
    return want


def kernel(x_prompt, x_sample, norm1_g, w_in, pool_w, pool_scale, conv_w, conv_b, filt_w1, filt_b1, filt_f1, filt_w2, filt_b2, filt_f2, filt_w3, hyena_bias, w_pool_out, w_hyena_out, w_o, norm2_g, peer_wq, peer_keys, peer_u, peer_v, normf_g):
    assert norm1_g.shape[0] == 1, "single-layer block"
    assert x_prompt.shape[1:] == x_sample.shape[1:]
    nb_p, seq_len, d = x_prompt.shape
    d_pool = pool_scale.shape[-1]
    d_hy = conv_b.shape[-1]
    d_h = d_hy // 3
    x = jnp.concatenate([x_prompt, x_sample], axis=0).reshape(-1, d)
    t = x.shape[0]
    tm = _pick_tile(seq_len, 512)

    ahy, sg = _inproj(x, norm1_g[0], w_in[0].astype(BF16), d_pool + d_hy, tm)
    p1, x0, z, zb = _local(ahy, sg, pool_w[0].astype(BF16), pool_scale[0], conv_w[0], conv_b[0],
                           w_pool_out[0].astype(BF16), seq_len, d_pool, d_hy, tm)
    ks, kd = _filters(filt_w1[0], filt_b1[0], filt_f1[0], filt_w2[0], filt_b2[0], filt_f2[0], filt_w3[0],
                      seq_len, d_h)
    cmat, smat, stmat = _dftgen(seq_len)
    kr, ki = _spectrum(cmat, smat, ks, kd, tm)
    hyb = _long_conv(cmat, smat, stmat, zb, z, x0, kr, ki, hyena_bias[0], seq_len, tm)

    keys = peer_keys[0]
    keys_bf = keys.reshape((-1,) + keys.shape[2:]).astype(BF16)
    h, hn, idx, gate = _route(hyb, p1, sg, x, w_hyena_out[0].astype(BF16), w_o[0].astype(BF16), norm2_g[0],
                              peer_wq[0].astype(BF16), keys_bf, _pick_tile(seq_len, 256))

    tok_shape = (t, d // LANES, LANES)
    tb = _pick_tile(seq_len, 256)
    act = _peer_u(idx, hn.reshape(tok_shape), gate, _pack_table(peer_u[0]), tb)
    y = _peer_v(idx, act, h.reshape(tok_shape), normf_g.reshape((1,) + tok_shape[1:]), _pack_table(peer_v[0]), tb)
    y = y.reshape(-1, seq_len, d)
    return (y[:nb_p], y[nb_p:])
```

```python
import functools
import math

import jax
import jax.numpy as jnp
import numpy as np
from jax import lax
from jax.experimental import pallas as pl
from jax.experimental.pallas import tpu as pltpu

F32 = jnp.float32
BF16 = jnp.bfloat16
I32 = jnp.int32

EPS = 1e-6
POOL_WINDOWS = (2, 4, 8, 16)
POOL_HALO = 8
PEER_HEADS = 8
PEER_NKEYS = 128
PEER_TOPK = 16
PEER_SLOTS = PEER_HEADS * PEER_TOPK
FILTER_EMB = 33
DECAY_TARGET = 1e-2
FAST_DECAY_PCT = 0.3
SLOW_DECAY_PCT = 1.5
LANES = 128
SUBLANES = 8
HI_MASK = -65536
MIB = 1024 * 1024


def _params(sem, vmem_mib):
    return pltpu.CompilerParams(dimension_semantics=sem, vmem_limit_bytes=vmem_mib * MIB)


def _dot(a, b):
    return jnp.dot(a, b, preferred_element_type=F32)


def _dot_nt(a, b):
    return lax.dot_general(a, b, (((1,), (1,)), ((), ())), preferred_element_type=F32)


def _dot_f32(a, b):
    return jnp.dot(a, b, preferred_element_type=F32, precision=lax.Precision.HIGHEST)


def _rms(x, g):
    return x * lax.rsqrt(jnp.mean(x * x, axis=-1, keepdims=True) + EPS) * g


def _pack_kernel(t_ref, o_ref):
    x = t_ref[...]
    for r in range(x.shape[1] // (2 * LANES)):
        lo = pltpu.bitcast(x[:, (2 * r) * LANES:(2 * r + 1) * LANES].astype(BF16).astype(F32), I32)
        hi = pltpu.bitcast(x[:, (2 * r + 1) * LANES:(2 * r + 2) * LANES].astype(BF16).astype(F32), I32)
        o_ref[:, r * LANES:(r + 1) * LANES] = hi | lax.shift_right_logical(lo, 16)


def _pack_table(tab):
    e, d = tab.shape
    rows = 512
    packed = pl.pallas_call(
        _pack_kernel,
        grid=(e // rows,),
        in_specs=[pl.BlockSpec((rows, d), lambda i: (i, 0))],
        out_specs=pl.BlockSpec((rows, d // 2), lambda i: (i, 0)),
        out_shape=jax.ShapeDtypeStruct((e, d // 2), I32),
        compiler_params=_params(("parallel",), 32),
        name="pack_table",
    )(tab)
    return packed.reshape(e * (d // 2) // LANES, LANES)


def _inproj_kernel(x_ref, g_ref, w_ref, ahy_ref, sg_ref, *, n_ahy, chunk):
    xn = _rms(x_ref[...], g_ref[...]).astype(BF16)
    for j in range(n_ahy // chunk):
        ahy_ref[:, j * chunk:(j + 1) * chunk] = _dot(xn, w_ref[:, j * chunk:(j + 1) * chunk])
    n_g = w_ref.shape[1] - n_ahy
    for j in range(n_g // chunk):
        g = _dot(xn, w_ref[:, n_ahy + j * chunk:n_ahy + (j + 1) * chunk])
        sg_ref[:, j * chunk:(j + 1) * chunk] = 1.0 / (1.0 + jnp.exp(-g))


def _inproj(x, g, w_bf, n_ahy, tm):
    t, d = x.shape
    n = w_bf.shape[1]
    return pl.pallas_call(
        functools.partial(_inproj_kernel, n_ahy=n_ahy, chunk=min(1024, n_ahy, n - n_ahy)),
        grid=(t // tm,),
        in_specs=[pl.BlockSpec((tm, d), lambda i: (i, 0)),
                  pl.BlockSpec((1, d), lambda i: (0, 0)),
                  pl.BlockSpec((d, n), lambda i: (0, 0))],
        out_specs=[pl.BlockSpec((tm, n_ahy), lambda i: (i, 0)),
                   pl.BlockSpec((tm, n - n_ahy), lambda i: (i, 0))],
        out_shape=[jax.ShapeDtypeStruct((t, n_ahy), F32), jax.ShapeDtypeStruct((t, n - n_ahy), F32)],
        compiler_params=_params(("parallel",), 56),
        name="inproj",
    )(x, g.reshape(1, d), w_bf)


def _local_kernel(ahy_ref, prev_ref, next_ref, sga_ref, pw_ref, ps_ref, cw_ref, cb_ref, wpo_ref,
                  p1_ref, x0_ref, z_ref, zb_ref, *, seq_len, d_pool, d_hy):
    tm = ahy_ref.shape[0]
    ext_rows = tm + 2 * POOL_HALO
    tiles_per_seq = seq_len // tm
    j = pl.program_id(0) % tiles_per_seq
    prev = jnp.where(j == 0, 0.0, prev_ref[...])
    nxt = jnp.where(j == tiles_per_seq - 1, 0.0, next_ref[...])
    pos = j * tm + lax.broadcasted_iota(I32, (tm, 1), 0)

    def ext(lo, hi):
        return jnp.concatenate([prev[:, lo:hi], ahy_ref[:, lo:hi], nxt[:, lo:hi]], axis=0)

    def shifted(v, k):
        return pltpu.roll(v, k % ext_rows, axis=0)

    group = d_pool // len(POOL_WINDOWS)
    mixed = []
    for g, w in enumerate(POOL_WINDOWS):
        half = w // 2
        e = ext(g * group, (g + 1) * group)
        win = e + shifted(e, 1)
        span = 1
        while span < half:
            win = shifted(win, span) + shifted(win, -span)
            span *= 2
        win = win[POOL_HALO:POOL_HALO + tm]
        cnt = (jnp.minimum(pos + half, seq_len) - jnp.maximum(pos - half, 0)).astype(F32)
        pooled = win / cnt - ahy_ref[:, g * group:(g + 1) * group]
        mixed.append(_dot(pooled.astype(BF16), pw_ref[g]))
    pm = jnp.concatenate(mixed, axis=1) * ps_ref[...]
    p1_ref[...] = sga_ref[...] * _dot(pm.astype(BF16), wpo_ref[...])

    d_br = d_hy // 3
    branches = []
    for b in range(3):
        lo, hi = d_pool + b * d_br, d_pool + (b + 1) * d_br
        e = ext(lo, hi)
        c0, c1 = b * d_br, (b + 1) * d_br
        uc = (shifted(e, 1)[POOL_HALO:POOL_HALO + tm] * cw_ref[0:1, c0:c1]
              + ahy_ref[:, lo:hi] * cw_ref[1:2, c0:c1]
              + shifted(e, -1)[POOL_HALO:POOL_HALO + tm] * cw_ref[2:3, c0:c1]
              + cb_ref[:, c0:c1])
        branches.append(uc)
    x0, x1, v = branches
    z = v * x1
    x0_ref[...] = x0
    z_ref[...] = z
    zb_ref[...] = z.astype(BF16)


def _local(ahy, sg, pool_w_bf, pool_scale, conv_w, conv_b, w_pool_out_bf, seq_len, d_pool, d_hy, tm):
    t, n_ahy = ahy.shape
    d_model = w_pool_out_bf.shape[1]
    d_br = d_hy // 3
    hb = tm // POOL_HALO
    n_halo_blocks = t // POOL_HALO
    return pl.pallas_call(
        functools.partial(_local_kernel, seq_len=seq_len, d_pool=d_pool, d_hy=d_hy),
        grid=(t // tm,),
        in_specs=[pl.BlockSpec((tm, n_ahy), lambda i: (i, 0)),
                  pl.BlockSpec((POOL_HALO, n_ahy), lambda i: (jnp.maximum(i * hb - 1, 0), 0)),
                  pl.BlockSpec((POOL_HALO, n_ahy), lambda i: (jnp.minimum((i + 1) * hb, n_halo_blocks - 1), 0)),
                  pl.BlockSpec((tm, d_model), lambda i: (i, 0)),
                  pl.BlockSpec(pool_w_bf.shape, lambda i: (0, 0, 0)),
                  pl.BlockSpec((1, d_pool), lambda i: (0, 0)),
                  pl.BlockSpec(conv_w.shape, lambda i: (0, 0)),
                  pl.BlockSpec((1, d_hy), lambda i: (0, 0)),
                  pl.BlockSpec(w_pool_out_bf.shape, lambda i: (0, 0))],
        out_specs=[pl.BlockSpec((tm, d_model), lambda i: (i, 0)),
                   pl.BlockSpec((tm, d_br), lambda i: (i, 0)),
                   pl.BlockSpec((tm, d_br), lambda i: (i, 0)),
                   pl.BlockSpec((tm, d_br), lambda i: (i, 0))],
        out_shape=[jax.ShapeDtypeStruct((t, d_model), F32),
                   jax.ShapeDtypeStruct((t, d_br), F32),
                   jax.ShapeDtypeStruct((t, d_br), F32),
                   jax.ShapeDtypeStruct((t, d_br), BF16)],
        compiler_params=_params(("parallel",), 56),
        name="local_mix",
    )(ahy, ahy, ahy, sg, pool_w_bf, pool_scale.reshape(1, d_pool), conv_w, conv_b.reshape(1, d_hy), w_pool_out_bf)


def _filter_kernel(freq_ref, w1_ref, b1_ref, f1_ref, w2_ref, b2_ref, f2_ref, w3_ref, dl_ref, ks_ref, kd_ref,
                   *, seq_len):
    tl = ks_ref.shape[0]
    d_h = ks_ref.shape[1]
    i = (pl.program_id(0) * tl + lax.broadcasted_iota(I32, (tl, 1), 0)).astype(F32)
    t = i / (seq_len - 1.0)
    wpos = (2.0 * math.pi) * i / seq_len
    lane = lax.broadcasted_iota(I32, (tl, LANES), 1)
    bands = (FILTER_EMB - 1) // 2
    ang = freq_ref[...] * wpos
    feat = jnp.where(lane == 0, t,
                     jnp.where(lane <= bands, jnp.cos(ang),
                               jnp.where(lane <= 2 * bands, -jnp.sin(ang), 0.0)))
    h = jnp.sin(f1_ref[...] * (_dot_f32(feat, w1_ref[...]) + b1_ref[...]))
    h = jnp.sin(f2_ref[...] * (_dot_f32(h, w2_ref[...]) + b2_ref[...]))
    h = _dot_f32(h, w3_ref[...])
    decay = jnp.exp(-t * jnp.abs(dl_ref[...]))
    kf = h[:, :d_h] * decay
    kb = h[:, d_h:] * decay
    ks_ref[...] = kf + kb
    kd_ref[...] = kf - kb


def _filters(w1, b1, f1, w2, b2, f2, w3, seq_len, d_h):
    hid = w1.shape[1]
    bands = (FILTER_EMB - 1) // 2
    freqs = np.zeros((1, LANES), np.float32)
    fr = np.linspace(1e-4, bands - 1, bands, dtype=np.float32)
    freqs[0, 1:1 + bands] = fr
    freqs[0, 1 + bands:1 + 2 * bands] = fr
    w1p = jnp.zeros((LANES, hid), F32).at[:FILTER_EMB].set(w1)
    max_decay = math.log(DECAY_TARGET) / FAST_DECAY_PCT
    min_decay = math.log(DECAY_TARGET) / SLOW_DECAY_PCT
    deltas = np.linspace(min_decay, max_decay, d_h, dtype=np.float32).reshape(1, d_h)
    tl = min(512, seq_len)
    full = lambda shape: pl.BlockSpec(shape, lambda i: tuple(0 for _ in shape))
    return pl.pallas_call(
        functools.partial(_filter_kernel, seq_len=seq_len),
        grid=(seq_len // tl,),
        in_specs=[full((1, LANES)), full((LANES, hid)), full((1, hid)), full((1, hid)),
                  full((hid, hid)), full((1, hid)), full((1, hid)), full((hid, 2 * d_h)), full((1, d_h))],
        out_specs=[pl.BlockSpec((tl, d_h), lambda i: (i, 0)), pl.BlockSpec((tl, d_h), lambda i: (i, 0))],
        out_shape=[jax.ShapeDtypeStruct((seq_len, d_h), F32), jax.ShapeDtypeStruct((seq_len, d_h), F32)],
        compiler_params=_params(("parallel",), 32),
        name="hyena_filters",
    )(jnp.asarray(freqs), w1p, b1.reshape(1, hid), f1.reshape(1, hid), w2, b2.reshape(1, hid),
      f2.reshape(1, hid), w3, jnp.asarray(deltas))


def _dftgen_kernel(c_ref, s_ref, st_ref, *, n_fft):
    tr, n = c_ref.shape
    row = pl.program_id(0) * tr + lax.broadcasted_iota(I32, (tr, n), 0)
    col = lax.broadcasted_iota(I32, (tr, n), 1)
    ang = ((row * col) & (n_fft - 1)).astype(F32) * (2.0 * math.pi / n_fft)
    c_ref[...] = jnp.cos(ang).astype(BF16)
    ms = -jnp.sin(ang)
    sign_col = (1 - 2 * (col & 1)).astype(F32)
    sign_row = (1 - 2 * (row & 1)).astype(F32)
    s_ref[...] = jnp.where(row == 0, sign_col, ms).astype(BF16)
    st_ref[...] = jnp.where(col == 0, sign_row, ms).astype(BF16)


def _dftgen(seq_len):
    tr = min(256, seq_len)
    spec = pl.BlockSpec((tr, seq_len), lambda i: (i, 0))
    shp = jax.ShapeDtypeStruct((seq_len, seq_len), BF16)
    return pl.pallas_call(
        functools.partial(_dftgen_kernel, n_fft=2 * seq_len),
        grid=(seq_len // tr,),
        out_specs=[spec, spec, spec],
        out_shape=[shp, shp, shp],
        compiler_params=_params(("parallel",), 48),
        name="dft_matrices",
    )()


def _spectrum_kernel(c_ref, s_ref, ks_ref, kd_ref, kr_ref, ki_ref):
    ks = ks_ref[...]
    kr_ref[...] = _dot(c_ref[...], ks.astype(BF16))
    ki = _dot(s_ref[...], kd_ref[...].astype(BF16))
    tf = ki.shape[0]
    row = lax.broadcasted_iota(I32, (tf, 1), 0)
    sign = (1 - 2 * (lax.broadcasted_iota(I32, (ks.shape[0], 1), 0) & 1)).astype(F32)
    nyq = jnp.sum(ks * sign, axis=0, keepdims=True)
    first = jnp.logical_and(pl.program_id(0) == 0, row == 0)
    ki_ref[...] = jnp.where(first, nyq, ki)


def _spectrum(cmat, smat, ks, kd, tf):
    n, d_h = ks.shape
    return pl.pallas_call(
        _spectrum_kernel,
        grid=(n // tf,),
        in_specs=[pl.BlockSpec((tf, n), lambda i: (i, 0)), pl.BlockSpec((tf, n), lambda i: (i, 0)),
                  pl.BlockSpec((n, d_h), lambda i: (0, 0)), pl.BlockSpec((n, d_h), lambda i: (0, 0))],
        out_specs=[pl.BlockSpec((tf, d_h), lambda i: (i, 0)), pl.BlockSpec((tf, d_h), lambda i: (i, 0))],
        out_shape=[jax.ShapeDtypeStruct((n, d_h), F32), jax.ShapeDtypeStruct((n, d_h), F32)],
        compiler_params=_params(("parallel",), 56),
        name="filter_spectrum",
    )(cmat, smat, ks, kd)


def _fwd_kernel(c_ref, s_ref, z_ref, kr_ref, ki_ref, yr_ref, yi_ref, *, n_fft):
    z = z_ref[...]
    zr = _dot(c_ref[...], z)
    zi = _dot(s_ref[...], z)
    kr = kr_ref[...]
    ki = ki_ref[...]
    row = lax.broadcasted_iota(I32, (zr.shape[0], 1), 0)
    first = jnp.logical_and(pl.program_id(0) == 0, row == 0)
    yr = jnp.where(first, zr * kr, zr * kr - zi * ki)
    yi = jnp.where(first, zi * ki, zr * ki + zi * kr)
    scale = jnp.where(first, 1.0 / n_fft, 2.0 / n_fft)
    yr_ref[...] = (yr * scale).astype(BF16)
    yi_ref[...] = (yi * scale).astype(BF16)


def _inv_kernel(c_ref, st_ref, yr_ref, yi_ref, z_ref, x0_ref, bias_ref, o_ref):
    y = _dot(c_ref[...], yr_ref[...]) + _dot(st_ref[...], yi_ref[...])
    o_ref[...] = ((y + z_ref[...] * bias_ref[...]) * x0_ref[...]).astype(BF16)


def _long_conv(cmat, smat, stmat, zb, z, x0, kr, ki, bias, seq_len, tf):
    t, d_h = z.shape
    nb = t // seq_len
    nf = seq_len // tf
    mat_spec = pl.BlockSpec((tf, seq_len), lambda f, b: (f, 0))
    seq_spec = pl.BlockSpec((seq_len, d_h), lambda f, b: (b, 0))
    tile_spec = pl.BlockSpec((tf, d_h), lambda f, b: (b * nf + f, 0))
    k_spec = pl.BlockSpec((tf, d_h), lambda f, b: (f, 0))
    yr, yi = pl.pallas_call(
        functools.partial(_fwd_kernel, n_fft=2 * seq_len),
        grid=(nf, nb),
        in_specs=[mat_spec, mat_spec, seq_spec, k_spec, k_spec],
        out_specs=[tile_spec, tile_spec],
        out_shape=[jax.ShapeDtypeStruct((t, d_h), BF16), jax.ShapeDtypeStruct((t, d_h), BF16)],
        compiler_params=_params(("parallel", "parallel"), 56),
        name="conv_fwd_dft",
    )(cmat, smat, zb, kr, ki)
    return pl.pallas_call(
        _inv_kernel,
        grid=(nf, nb),
        in_specs=[mat_spec, mat_spec, seq_spec, seq_spec, tile_spec, tile_spec,
                  pl.BlockSpec((1, d_h), lambda f, b: (0, 0))],
        out_specs=tile_spec,
        out_shape=jax.ShapeDtypeStruct((t, d_h), BF16),
        compiler_params=_params(("parallel", "parallel"), 56),
        name="conv_inv_dft",
    )(cmat, stmat, yr, yi, z, x0, bias.reshape(1, d_h))


def _top16(s, payload=None):
    n_rows = s.shape[0]
    rid = lax.broadcasted_iota(I32, s.shape, 0).astype(F32)
    vals, ids = [], []
    for _ in range(PEER_TOPK):
        m = jnp.max(s, axis=0, keepdims=True)
        pos = jnp.min(jnp.where(s == m, rid, float(n_rows)), axis=0, keepdims=True)
        hit = rid == pos
        vals.append(m)
        ids.append(pos if payload is None else jnp.max(jnp.where(hit, payload, -1.0), axis=0, keepdims=True))
        s = jnp.where(hit, -jnp.inf, s)
    return jnp.concatenate(vals, axis=0), jnp.concatenate(ids, axis=0)


def _route_kernel(hyb_ref, p1_ref, sgb_ref, x_ref, who_ref, wo_ref, g2_ref, wq_ref, keys_ref,
                  h_ref, hn_ref, idx_ref, gate_ref, q_scr, idx_scr, gate_scr, cand_scr, cidx_scr, *, idx_scale):
    yb = _dot(hyb_ref[...], who_ref[...])
    mixed = p1_ref[...] + sgb_ref[...] * yb
    h = x_ref[...] + _dot(mixed.astype(BF16), wo_ref[...])
    h_ref[...] = h
    hn = _rms(h, g2_ref[...])
    hn_ref[...] = hn
    q = _dot(hn.astype(BF16), wq_ref[...])
    n_half = 2 * PEER_HEADS
    dh = q.shape[1] // n_half
    for j in range(n_half):
        q_scr[j] = q[:, j * dh:(j + 1) * dh].astype(BF16)

    def head(hd, carry):
        sv0, si0 = _top16(_dot_nt(keys_ref[2 * hd], q_scr[2 * hd]))
        sv1, si1 = _top16(_dot_nt(keys_ref[2 * hd + 1], q_scr[2 * hd + 1]))
        cand_scr[...] = jnp.full(cand_scr.shape, -jnp.inf, F32)
        cidx_scr[...] = jnp.full(cidx_scr.shape, -1.0, F32)
        off = 0
        for a in range(PEER_TOPK):
            nb = PEER_TOPK // (a + 1)
            cand_scr[off:off + nb, :] = sv0[a:a + 1] + sv1[:nb]
            cidx_scr[off:off + nb, :] = (si0[a:a + 1] * float(PEER_NKEYS) + si1[:nb]) * float(idx_scale)
            off += nb
        best, eid = _top16(cand_scr[...], cidx_scr[...])
        ex = jnp.exp(best - best[0:1])
        r0 = pl.multiple_of(hd * PEER_TOPK, PEER_TOPK)
        idx_scr[pl.ds(r0, PEER_TOPK), :] = eid.astype(I32)
        gate_scr[pl.ds(r0, PEER_TOPK), :] = ex / jnp.sum(ex, axis=0, keepdims=True)
        return carry

    lax.fori_loop(0, PEER_HEADS, head, 0)
    idx_ref[...] = idx_scr[...].T
    gate_ref[...] = gate_scr[...].T


def _route(hyb, p1, sg, x, who_bf, wo_bf, g2, wq_bf, keys_bf, tm, idx_scale):
    t, d = x.shape
    d_h = hyb.shape[1]
    n_half, nk, dh = keys_bf.shape
    n_cand = sum(PEER_TOPK // (a + 1) for a in range(PEER_TOPK))
    n_cand = -(-n_cand // SUBLANES) * SUBLANES
    tok = lambda w: pl.BlockSpec((tm, w), lambda i: (i, 0))
    full = lambda a: pl.BlockSpec(a.shape, lambda i: tuple(0 for _ in a.shape))
    return pl.pallas_call(
        functools.partial(_route_kernel, idx_scale=idx_scale),
        grid=(t // tm,),
        in_specs=[tok(d_h), tok(d), pl.BlockSpec((tm, d), lambda i: (i, 1)), tok(d),
                  full(who_bf), full(wo_bf), pl.BlockSpec((1, d), lambda i: (0, 0)), full(wq_bf), full(keys_bf)],
        out_specs=[tok(d), tok(d), tok(PEER_SLOTS), tok(PEER_SLOTS)],
        out_shape=[jax.ShapeDtypeStruct((t, d), F32), jax.ShapeDtypeStruct((t, d), F32),
                   jax.ShapeDtypeStruct((t, PEER_SLOTS), I32), jax.ShapeDtypeStruct((t, PEER_SLOTS), F32)],
        scratch_shapes=[pltpu.VMEM((n_half, tm, dh), BF16),
                        pltpu.VMEM((PEER_SLOTS, tm), I32),
                        pltpu.VMEM((PEER_SLOTS, tm), F32),
                        pltpu.VMEM((n_cand, tm), F32),
                        pltpu.VMEM((n_cand, tm), F32)],
        compiler_params=_params(("parallel",), 56),
        name="route",
    )(hyb, p1, sg, x, who_bf, wo_bf, g2.reshape(1, d), wq_bf, keys_bf)


def _gather_rows(tab_ref, idx_ref, t, raw_ref):
    rows = raw_ref.shape[0] // PEER_SLOTS
    for e in range(PEER_SLOTS):
        r = pl.multiple_of(idx_ref[t, e], rows)
        raw_ref[e * rows:(e + 1) * rows, :] = tab_ref[pl.ds(r, rows), :]


def _split2(x):
    hi = x.astype(BF16)
    return hi, (x - hi.astype(F32)).astype(BF16)


def _gelu(x):
    return 0.5 * x * (1.0 + jnp.tanh(math.sqrt(2.0 / math.pi) * (x + 0.044715 * (x * x * x))))


TOKENS_PER_STEP = 16
N_GATHER_BUFS = 2


def _token_pipeline(tb, gather, compute, bufs):
    n = len(bufs)
    assert TOKENS_PER_STEP % n == 0 and tb % TOKENS_PER_STEP == 0
    gather(0, bufs[0])

    def step(i, carry):
        t0 = i * TOKENS_PER_STEP
        for j in range(TOKENS_PER_STEP):
            gather(jnp.minimum(t0 + j + 1, tb - 1), bufs[(j + 1) % n])
            compute(t0 + j, bufs[j % n])
        return carry

    lax.fori_loop(0, tb // TOKENS_PER_STEP, step, 0)


def _peer_u_kernel(idx_ref, x_ref, gate_ref, rep_ref, tab_ref, arep_ref, pre_ref, *bufs):
    tb = x_ref.shape[0]
    raws, stages = bufs[:N_GATHER_BUFS], bufs[N_GATHER_BUFS:]
    rows = raws[0].shape[0] // PEER_SLOTS
    per_vreg = SUBLANES // rows
    ones = jnp.ones((SUBLANES, LANES), BF16)

    def gather(t, bufs):
        _gather_rows(tab_ref, idx_ref, t, bufs[0])

    def compute(t, bufs):
        raw_ref, stage_ref = bufs
        x_lo = jnp.concatenate([x_ref[t, pl.ds(0, rows, stride=2), :]] * per_vreg, axis=0)
        x_hi = jnp.concatenate([x_ref[t, pl.ds(1, rows, stride=2), :]] * per_vreg, axis=0)
        for k in range(PEER_SLOTS // per_vreg):
            w = raw_ref[k * SUBLANES:(k + 1) * SUBLANES, :]
            p = pltpu.bitcast(w << 16, F32) * x_lo + pltpu.bitcast(w & HI_MASK, F32) * x_hi
            step = 1
            while step < rows:
                p = p + pltpu.roll(p, SUBLANES - step, axis=0)
                step *= 2
            for j in range(per_vreg):
                e = k * per_vreg + j
                stage_ref[e:e + 1, :] = p[j * rows:j * rows + 1]
        p_hi, p_lo = _split2(stage_ref[...])
        sums = _dot_nt(ones, p_hi) + _dot_nt(ones, p_lo)
        pre_ref[pl.ds(t, 1), :] = sums[0:1]

    _token_pipeline(tb, gather, compute, list(zip(raws, stages)))
    act = _gelu(pre_ref[...]) * gate_ref[...]
    a1 = act.astype(BF16)
    r1 = act - a1.astype(F32)
    a2, a3 = _split2(r1)
    rep = rep_ref[...]
    arep_ref[...] = _dot(a1, rep) + _dot(a2, rep) + _dot(a3, rep)


def _peer_u(idx, x3, gate, tab, tb):
    t, nch, _ = x3.shape
    rows = tab.shape[0] // (PEER_NKEYS * PEER_NKEYS)
    rep = np.zeros((PEER_SLOTS, PEER_SLOTS * nch), np.float32)
    for e in range(PEER_SLOTS):
        rep[e, e * nch:(e + 1) * nch] = 1.0
    return pl.pallas_call(
        _peer_u_kernel,
        grid=(t // tb,),
        in_specs=[pl.BlockSpec((tb, PEER_SLOTS), lambda i: (i, 0), memory_space=pltpu.SMEM),
                  pl.BlockSpec((tb, nch, LANES), lambda i: (i, 0, 0)),
                  pl.BlockSpec((tb, PEER_SLOTS), lambda i: (i, 0)),
                  pl.BlockSpec(rep.shape, lambda i: (0, 0)),
                  pl.BlockSpec(tab.shape, lambda i: (0, 0), pipeline_mode=pl.Buffered(1))],
        out_specs=pl.BlockSpec((tb, PEER_SLOTS * nch), lambda i: (i, 0)),
        out_shape=jax.ShapeDtypeStruct((t, PEER_SLOTS * nch), F32),
        scratch_shapes=([pltpu.VMEM((tb, PEER_SLOTS), F32)]
                        + [pltpu.VMEM((PEER_SLOTS * rows, LANES), I32)] * N_GATHER_BUFS
                        + [pltpu.VMEM((PEER_SLOTS, LANES), F32)] * N_GATHER_BUFS),
        compiler_params=_params(("arbitrary",), 48),
        name="peer_u",
    )(idx, x3, gate, jnp.asarray(rep, BF16), tab)


def _peer_v_kernel(idx_ref, arep_ref, h_ref, g_ref, tab_ref, y_ref, *raws):
    tb, nch, _ = h_ref.shape
    shape = (nch, arep_ref.shape[1])
    own_chunk = (lax.broadcasted_iota(I32, shape, 1) % nch) == lax.broadcasted_iota(I32, shape, 0)

    def gather(t, raw_ref):
        _gather_rows(tab_ref, idx_ref, t, raw_ref)

    def compute(t, raw_ref):
        vals = pltpu.bitcast(raw_ref[...], BF16)
        w_hi, w_lo = _split2(jnp.where(own_chunk, arep_ref[pl.ds(t, 1), :], 0.0))
        y_ref[t] = _dot(w_hi, vals) + _dot(w_lo, vals)

    _token_pipeline(tb, gather, compute, raws)
    y = h_ref[...] + y_ref[...]
    ms = jnp.sum(jnp.sum(y * y, axis=2, keepdims=True), axis=1, keepdims=True) * (1.0 / (y.shape[1] * y.shape[2]))
    y_ref[...] = y * lax.rsqrt(ms + EPS) * g_ref[...]


def _peer_v(idx, arep, h3, g3, tab, tb):
    t = h3.shape[0]
    rows = tab.shape[0] // (PEER_NKEYS * PEER_NKEYS)
    blk = pl.BlockSpec((tb,) + h3.shape[1:], lambda i: (i, 0, 0))
    return pl.pallas_call(
        _peer_v_kernel,
        grid=(t // tb,),
        in_specs=[pl.BlockSpec((tb, PEER_SLOTS), lambda i: (i, 0), memory_space=pltpu.SMEM),
                  pl.BlockSpec((tb, arep.shape[1]), lambda i: (i, 0)),
                  blk,
                  pl.BlockSpec(g3.shape, lambda i: (0, 0, 0)),
                  pl.BlockSpec(tab.shape, lambda i: (0, 0), pipeline_mode=pl.Buffered(1))],
        out_specs=blk,
        out_shape=jax.ShapeDtypeStruct(h3.shape, F32),
        scratch_shapes=[pltpu.VMEM((PEER_SLOTS * rows, LANES), I32)] * N_GATHER_BUFS,
        compiler_params=_params(("arbitrary",), 48),
        name="peer_v",
    )(idx, arep, h3, g3, tab)


def _pick_tile(n, want):
    while n % want:
        want //= 2
    return want


def kernel(x_prompt, x_sample, norm1_g, w_in, pool_w, pool_scale, conv_w, conv_b, filt_w1, filt_b1, filt_f1, filt_w2, filt_b2, filt_f2, filt_w3, hyena_bias, w_pool_out, w_hyena_out, w_o, norm2_g, peer_wq, peer_keys, peer_u, peer_v, normf_g):
    assert norm1_g.shape[0] == 1, "single-layer block"
    assert x_prompt.shape[1:] == x_sample.shape[1:]
    nb_p, seq_len, d = x_prompt.shape
    d_pool = pool_scale.shape[-1]
    d_hy = conv_b.shape[-1]
    d_h = d_hy // 3
    x = jnp.concatenate([x_prompt, x_sample], axis=0).reshape(-1, d)
    t = x.shape[0]
    tm = _pick_tile(seq_len, 512)

    ahy, sg = _inproj(x, norm1_g[0], w_in[0].astype(BF16), d_pool + d_hy, tm)
    p1, x0, z, zb = _local(ahy, sg, pool_w[0].astype(BF16), pool_scale[0], conv_w[0], conv_b[0],
                           w_pool_out[0].astype(BF16), seq_len, d_pool, d_hy, tm)
    ks, kd = _filters(filt_w1[0], filt_b1[0], filt_f1[0], filt_w2[0], filt_b2[0], filt_f2[0], filt_w3[0],
                      seq_len, d_h)
    cmat, smat, stmat = _dftgen(seq_len)
    kr, ki = _spectrum(cmat, smat, ks, kd, tm)
    hyb = _long_conv(cmat, smat, stmat, zb, z, x0, kr, ki, hyena_bias[0], seq_len, tm)

    keys = peer_keys[0]
    keys_bf = keys.reshape((-1,) + keys.shape[2:]).astype(BF16)
    h, hn, idx, gate = _route(hyb, p1, sg, x, w_hyena_out[0].astype(BF16), w_o[0].astype(BF16), norm2_g[0],
                              peer_wq[0].astype(BF16), keys_bf, _pick_tile(seq_len, 256),
                              idx_scale=d // (2 * LANES))

    tok_shape = (t, d // LANES, LANES)
    tb = _pick_tile(seq_len, 256)
    arep = _peer_u(idx, hn.reshape(tok_shape), gate, _pack_table(peer_u[0]), tb)
    y = _peer_v(idx, arep, h.reshape(tok_shape), normf_g.reshape((1,) + tok_shape[1:]), _pack_table(peer_v[0]), tb)
    y = y.reshape(-1, seq_len, d)
    return (y[:nb_p], y[nb_p:])
```

```python
import functools
import math

import jax
import jax.numpy as jnp
import numpy as np
from jax import lax
from jax.experimental import pallas as pl
from jax.experimental.pallas import tpu as pltpu

F32 = jnp.float32
BF16 = jnp.bfloat16
I32 = jnp.int32

EPS = 1e-6
POOL_WINDOWS = (2, 4, 8, 16)
POOL_HALO = 8
PEER_HEADS = 8
PEER_NKEYS = 128
PEER_TOPK = 16
PEER_SLOTS = PEER_HEADS * PEER_TOPK
FILTER_EMB = 33
DECAY_TARGET = 1e-2
FAST_DECAY_PCT = 0.3
SLOW_DECAY_PCT = 1.5
LANES = 128
SUBLANES = 8
HI_MASK = -65536
MIB = 1024 * 1024


def _params(sem, vmem_mib):
    return pltpu.CompilerParams(dimension_semantics=sem, vmem_limit_bytes=vmem_mib * MIB)


def _dot(a, b):
    return jnp.dot(a, b, preferred_element_type=F32)


def _dot_nt(a, b):
    return lax.dot_general(a, b, (((1,), (1,)), ((), ())), preferred_element_type=F32)


def _dot_f32(a, b):
    return jnp.dot(a, b, preferred_element_type=F32, precision=lax.Precision.HIGHEST)


def _rms(x, g):
    return x * lax.rsqrt(jnp.mean(x * x, axis=-1, keepdims=True) + EPS) * g


def _pack_kernel(t_ref, o_ref):
    x = t_ref[...]
    for r in range(x.shape[1] // (2 * LANES)):
        lo = pltpu.bitcast(x[:, (2 * r) * LANES:(2 * r + 1) * LANES].astype(BF16).astype(F32), I32)
        hi = pltpu.bitcast(x[:, (2 * r + 1) * LANES:(2 * r + 2) * LANES].astype(BF16).astype(F32), I32)
        o_ref[:, r * LANES:(r + 1) * LANES] = hi | lax.shift_right_logical(lo, 16)


def _pack_table(tab):
    e, d = tab.shape
    rows = 512
    packed = pl.pallas_call(
        _pack_kernel,
        grid=(e // rows,),
        in_specs=[pl.BlockSpec((rows, d), lambda i: (i, 0))],
        out_specs=pl.BlockSpec((rows, d // 2), lambda i: (i, 0)),
        out_shape=jax.ShapeDtypeStruct((e, d // 2), I32),
        compiler_params=_params(("parallel",), 32),
        name="pack_table",
    )(tab)
    return packed.reshape(e * (d // 2) // LANES, LANES)


def _inproj_kernel(x_ref, g_ref, w_ref, ahy_ref, sg_ref, *, n_ahy, chunk):
    xn = _rms(x_ref[...], g_ref[...]).astype(BF16)
    for j in range(n_ahy // chunk):
        ahy_ref[:, j * chunk:(j + 1) * chunk] = _dot(xn, w_ref[:, j * chunk:(j + 1) * chunk])
    n_g = w_ref.shape[1] - n_ahy
    for j in range(n_g // chunk):
        g = _dot(xn, w_ref[:, n_ahy + j * chunk:n_ahy + (j + 1) * chunk])
        sg_ref[:, j * chunk:(j + 1) * chunk] = 1.0 / (1.0 + jnp.exp(-g))


def _inproj(x, g, w_bf, n_ahy, tm):
    t, d = x.shape
    n = w_bf.shape[1]
    return pl.pallas_call(
        functools.partial(_inproj_kernel, n_ahy=n_ahy, chunk=min(1024, n_ahy, n - n_ahy)),
        grid=(t // tm,),
        in_specs=[pl.BlockSpec((tm, d), lambda i: (i, 0)),
                  pl.BlockSpec((1, d), lambda i: (0, 0)),
                  pl.BlockSpec((d, n), lambda i: (0, 0))],
        out_specs=[pl.BlockSpec((tm, n_ahy), lambda i: (i, 0)),
                   pl.BlockSpec((tm, n - n_ahy), lambda i: (i, 0))],
        out_shape=[jax.ShapeDtypeStruct((t, n_ahy), F32), jax.ShapeDtypeStruct((t, n - n_ahy), F32)],
        compiler_params=_params(("parallel",), 56),
        name="inproj",
    )(x, g.reshape(1, d), w_bf)


def _local_kernel(ahy_ref, prev_ref, next_ref, sga_ref, pw_ref, ps_ref, cw_ref, cb_ref, wpo_ref,
                  p1_ref, x0_ref, z_ref, zb_ref, *, seq_len, d_pool, d_hy):
    tm = ahy_ref.shape[0]
    ext_rows = tm + 2 * POOL_HALO
    tiles_per_seq = seq_len // tm
    j = pl.program_id(0) % tiles_per_seq
    prev = jnp.where(j == 0, 0.0, prev_ref[...])
    nxt = jnp.where(j == tiles_per_seq - 1, 0.0, next_ref[...])
    pos = j * tm + lax.broadcasted_iota(I32, (tm, 1), 0)

    def ext(lo, hi):
        return jnp.concatenate([prev[:, lo:hi], ahy_ref[:, lo:hi], nxt[:, lo:hi]], axis=0)

    def shifted(v, k):
        return pltpu.roll(v, k % ext_rows, axis=0)

    group = d_pool // len(POOL_WINDOWS)
    mixed = []
    for g, w in enumerate(POOL_WINDOWS):
        half = w // 2
        e = ext(g * group, (g + 1) * group)
        win = e + shifted(e, 1)
        span = 1
        while span < half:
            win = shifted(win, span) + shifted(win, -span)
            span *= 2
        win = win[POOL_HALO:POOL_HALO + tm]
        cnt = (jnp.minimum(pos + half, seq_len) - jnp.maximum(pos - half, 0)).astype(F32)
        pooled = win / cnt - ahy_ref[:, g * group:(g + 1) * group]
        mixed.append(_dot(pooled.astype(BF16), pw_ref[g]))
    pm = jnp.concatenate(mixed, axis=1) * ps_ref[...]
    p1_ref[...] = sga_ref[...] * _dot(pm.astype(BF16), wpo_ref[...])

    d_br = d_hy // 3
    branches = []
    for b in range(3):
        lo, hi = d_pool + b * d_br, d_pool + (b + 1) * d_br
        e = ext(lo, hi)
        c0, c1 = b * d_br, (b + 1) * d_br
        uc = (shifted(e, 1)[POOL_HALO:POOL_HALO + tm] * cw_ref[0:1, c0:c1]
              + ahy_ref[:, lo:hi] * cw_ref[1:2, c0:c1]
              + shifted(e, -1)[POOL_HALO:POOL_HALO + tm] * cw_ref[2:3, c0:c1]
              + cb_ref[:, c0:c1])
        branches.append(uc)
    x0, x1, v = branches
    z = v * x1
    x0_ref[...] = x0
    z_ref[...] = z
    zb_ref[...] = z.astype(BF16)


def _local(ahy, sg, pool_w_bf, pool_scale, conv_w, conv_b, w_pool_out_bf, seq_len, d_pool, d_hy, tm):
    t, n_ahy = ahy.shape
    d_model = w_pool_out_bf.shape[1]
    d_br = d_hy // 3
    hb = tm // POOL_HALO
    n_halo_blocks = t // POOL_HALO
    return pl.pallas_call(
        functools.partial(_local_kernel, seq_len=seq_len, d_pool=d_pool, d_hy=d_hy),
        grid=(t // tm,),
        in_specs=[pl.BlockSpec((tm, n_ahy), lambda i: (i, 0)),
                  pl.BlockSpec((POOL_HALO, n_ahy), lambda i: (jnp.maximum(i * hb - 1, 0), 0)),
                  pl.BlockSpec((POOL_HALO, n_ahy), lambda i: (jnp.minimum((i + 1) * hb, n_halo_blocks - 1), 0)),
                  pl.BlockSpec((tm, d_model), lambda i: (i, 0)),
                  pl.BlockSpec(pool_w_bf.shape, lambda i: (0, 0, 0)),
                  pl.BlockSpec((1, d_pool), lambda i: (0, 0)),
                  pl.BlockSpec(conv_w.shape, lambda i: (0, 0)),
                  pl.BlockSpec((1, d_hy), lambda i: (0, 0)),
                  pl.BlockSpec(w_pool_out_bf.shape, lambda i: (0, 0))],
        out_specs=[pl.BlockSpec((tm, d_model), lambda i: (i, 0)),
                   pl.BlockSpec((tm, d_br), lambda i: (i, 0)),
                   pl.BlockSpec((tm, d_br), lambda i: (i, 0)),
                   pl.BlockSpec((tm, d_br), lambda i: (i, 0))],
        out_shape=[jax.ShapeDtypeStruct((t, d_model), F32),
                   jax.ShapeDtypeStruct((t, d_br), F32),
                   jax.ShapeDtypeStruct((t, d_br), F32),
                   jax.ShapeDtypeStruct((t, d_br), BF16)],
        compiler_params=_params(("parallel",), 56),
        name="local_mix",
    )(ahy, ahy, ahy, sg, pool_w_bf, pool_scale.reshape(1, d_pool), conv_w, conv_b.reshape(1, d_hy), w_pool_out_bf)


def _filter_kernel(freq_ref, w1_ref, b1_ref, f1_ref, w2_ref, b2_ref, f2_ref, w3_ref, dl_ref, ks_ref, kd_ref,
                   *, seq_len):
    tl = ks_ref.shape[0]
    d_h = ks_ref.shape[1]
    i = (pl.program_id(0) * tl + lax.broadcasted_iota(I32, (tl, 1), 0)).astype(F32)
    t = i / (seq_len - 1.0)
    wpos = (2.0 * math.pi) * i / seq_len
    lane = lax.broadcasted_iota(I32, (tl, LANES), 1)
    bands = (FILTER_EMB - 1) // 2
    ang = freq_ref[...] * wpos
    feat = jnp.where(lane == 0, t,
                     jnp.where(lane <= bands, jnp.cos(ang),
                               jnp.where(lane <= 2 * bands, -jnp.sin(ang), 0.0)))
    h = jnp.sin(f1_ref[...] * (_dot_f32(feat, w1_ref[...]) + b1_ref[...]))
    h = jnp.sin(f2_ref[...] * (_dot_f32(h, w2_ref[...]) + b2_ref[...]))
    h = _dot_f32(h, w3_ref[...])
    decay = jnp.exp(-t * jnp.abs(dl_ref[...]))
    kf = h[:, :d_h] * decay
    kb = h[:, d_h:] * decay
    ks_ref[...] = kf + kb
    kd_ref[...] = kf - kb


def _filters(w1, b1, f1, w2, b2, f2, w3, seq_len, d_h):
    hid = w1.shape[1]
    bands = (FILTER_EMB - 1) // 2
    freqs = np.zeros((1, LANES), np.float32)
    fr = np.linspace(1e-4, bands - 1, bands, dtype=np.float32)
    freqs[0, 1:1 + bands] = fr
    freqs[0, 1 + bands:1 + 2 * bands] = fr
    w1p = jnp.zeros((LANES, hid), F32).at[:FILTER_EMB].set(w1)
    max_decay = math.log(DECAY_TARGET) / FAST_DECAY_PCT
    min_decay = math.log(DECAY_TARGET) / SLOW_DECAY_PCT
    deltas = np.linspace(min_decay, max_decay, d_h, dtype=np.float32).reshape(1, d_h)
    tl = min(512, seq_len)
    full = lambda shape: pl.BlockSpec(shape, lambda i: tuple(0 for _ in shape))
    return pl.pallas_call(
        functools.partial(_filter_kernel, seq_len=seq_len),
        grid=(seq_len // tl,),
        in_specs=[full((1, LANES)), full((LANES, hid)), full((1, hid)), full((1, hid)),
                  full((hid, hid)), full((1, hid)), full((1, hid)), full((hid, 2 * d_h)), full((1, d_h))],
        out_specs=[pl.BlockSpec((tl, d_h), lambda i: (i, 0)), pl.BlockSpec((tl, d_h), lambda i: (i, 0))],
        out_shape=[jax.ShapeDtypeStruct((seq_len, d_h), F32), jax.ShapeDtypeStruct((seq_len, d_h), F32)],
        compiler_params=_params(("parallel",), 32),
        name="hyena_filters",
    )(jnp.asarray(freqs), w1p, b1.reshape(1, hid), f1.reshape(1, hid), w2, b2.reshape(1, hid),
      f2.reshape(1, hid), w3, jnp.asarray(deltas))


def _dftgen_kernel(c_ref, s_ref, st_ref, *, n_fft):
    tr, n = c_ref.shape
    row = pl.program_id(0) * tr + lax.broadcasted_iota(I32, (tr, n), 0)
    col = lax.broadcasted_iota(I32, (tr, n), 1)
    ang = ((row * col) & (n_fft - 1)).astype(F32) * (2.0 * math.pi / n_fft)
    c_ref[...] = jnp.cos(ang).astype(BF16)
    ms = -jnp.sin(ang)
    sign_col = (1 - 2 * (col & 1)).astype(F32)
    sign_row = (1 - 2 * (row & 1)).astype(F32)
    s_ref[...] = jnp.where(row == 0, sign_col, ms).astype(BF16)
    st_ref[...] = jnp.where(col == 0, sign_row, ms).astype(BF16)


def _dftgen(seq_len):
    tr = min(256, seq_len)
    spec = pl.BlockSpec((tr, seq_len), lambda i: (i, 0))
    shp = jax.ShapeDtypeStruct((seq_len, seq_len), BF16)
    return pl.pallas_call(
        functools.partial(_dftgen_kernel, n_fft=2 * seq_len),
        grid=(seq_len // tr,),
        out_specs=[spec, spec, spec],
        out_shape=[shp, shp, shp],
        compiler_params=_params(("parallel",), 48),
        name="dft_matrices",
    )()


def _spectrum_kernel(c_ref, s_ref, ks_ref, kd_ref, kr_ref, ki_ref):
    ks = ks_ref[...]
    kr_ref[...] = _dot(c_ref[...], ks.astype(BF16))
    ki = _dot(s_ref[...], kd_ref[...].astype(BF16))
    tf = ki.shape[0]
    row = lax.broadcasted_iota(I32, (tf, 1), 0)
    sign = (1 - 2 * (lax.broadcasted_iota(I32, (ks.shape[0], 1), 0) & 1)).astype(F32)
    nyq = jnp.sum(ks * sign, axis=0, keepdims=True)
    first = jnp.logical_and(pl.program_id(0) == 0, row == 0)
    ki_ref[...] = jnp.where(first, nyq, ki)


def _spectrum(cmat, smat, ks, kd, tf):
    n, d_h = ks.shape
    return pl.pallas_call(
        _spectrum_kernel,
        grid=(n // tf,),
        in_specs=[pl.BlockSpec((tf, n), lambda i: (i, 0)), pl.BlockSpec((tf, n), lambda i: (i, 0)),
                  pl.BlockSpec((n, d_h), lambda i: (0, 0)), pl.BlockSpec((n, d_h), lambda i: (0, 0))],
        out_specs=[pl.BlockSpec((tf, d_h), lambda i: (i, 0)), pl.BlockSpec((tf, d_h), lambda i: (i, 0))],
        out_shape=[jax.ShapeDtypeStruct((n, d_h), F32), jax.ShapeDtypeStruct((n, d_h), F32)],
        compiler_params=_params(("parallel",), 56),
        name="filter_spectrum",
    )(cmat, smat, ks, kd)


def _fwd_kernel(c_ref, s_ref, z_ref, kr_ref, ki_ref, yr_ref, yi_ref, *, n_fft):
    z = z_ref[...]
    zr = _dot(c_ref[...], z)
    zi = _dot(s_ref[...], z)
    kr = kr_ref[...]
    ki = ki_ref[...]
    row = lax.broadcasted_iota(I32, (zr.shape[0], 1), 0)
    first = jnp.logical_and(pl.program_id(0) == 0, row == 0)
    yr = jnp.where(first, zr * kr, zr * kr - zi * ki)
    yi = jnp.where(first, zi * ki, zr * ki + zi * kr)
    scale = jnp.where(first, 1.0 / n_fft, 2.0 / n_fft)
    yr_ref[...] = (yr * scale).astype(BF16)
    yi_ref[...] = (yi * scale).astype(BF16)


def _inv_kernel(c_ref, st_ref, yr_ref, yi_ref, z_ref, x0_ref, bias_ref, o_ref):
    y = _dot(c_ref[...], yr_ref[...]) + _dot(st_ref[...], yi_ref[...])
    o_ref[...] = ((y + z_ref[...] * bias_ref[...]) * x0_ref[...]).astype(BF16)


def _long_conv(cmat, smat, stmat, zb, z, x0, kr, ki, bias, seq_len, tf):
    t, d_h = z.shape
    nb = t // seq_len
    nf = seq_len // tf
    mat_spec = pl.BlockSpec((tf, seq_len), lambda f, b: (f, 0))
    seq_spec = pl.BlockSpec((seq_len, d_h), lambda f, b: (b, 0))
    tile_spec = pl.BlockSpec((tf, d_h), lambda f, b: (b * nf + f, 0))
    k_spec = pl.BlockSpec((tf, d_h), lambda f, b: (f, 0))
    yr, yi = pl.pallas_call(
        functools.partial(_fwd_kernel, n_fft=2 * seq_len),
        grid=(nf, nb),
        in_specs=[mat_spec, mat_spec, seq_spec, k_spec, k_spec],
        out_specs=[tile_spec, tile_spec],
        out_shape=[jax.ShapeDtypeStruct((t, d_h), BF16), jax.ShapeDtypeStruct((t, d_h), BF16)],
        compiler_params=_params(("parallel", "parallel"), 56),
        name="conv_fwd_dft",
    )(cmat, smat, zb, kr, ki)
    return pl.pallas_call(
        _inv_kernel,
        grid=(nf, nb),
        in_specs=[mat_spec, mat_spec, seq_spec, seq_spec, tile_spec, tile_spec,
                  pl.BlockSpec((1, d_h), lambda f, b: (0, 0))],
        out_specs=tile_spec,
        out_shape=jax.ShapeDtypeStruct((t, d_h), BF16),
        compiler_params=_params(("parallel", "parallel"), 56),
        name="conv_inv_dft",
    )(cmat, stmat, yr, yi, z, x0, bias.reshape(1, d_h))


def _top16(s, payload=None):
    n_rows = s.shape[0]
    rid = lax.broadcasted_iota(I32, s.shape, 0).astype(F32)
    vals, ids = [], []
    for _ in range(PEER_TOPK):
        m = jnp.max(s, axis=0, keepdims=True)
        pos = jnp.min(jnp.where(s == m, rid, float(n_rows)), axis=0, keepdims=True)
        hit = rid == pos
        vals.append(m)
        ids.append(pos if payload is None else jnp.max(jnp.where(hit, payload, -1.0), axis=0, keepdims=True))
        s = jnp.where(hit, -jnp.inf, s)
    return jnp.concatenate(vals, axis=0), jnp.concatenate(ids, axis=0)


def _route_kernel(hyb_ref, p1_ref, sgb_ref, x_ref, who_ref, wo_ref, g2_ref, wq_ref, keys_ref,
                  h_ref, hn_ref, idx_ref, gate_ref, q_scr, idx_scr, gate_scr, cand_scr, cidx_scr, *, idx_scale):
    yb = _dot(hyb_ref[...], who_ref[...])
    mixed = p1_ref[...] + sgb_ref[...] * yb
    h = x_ref[...] + _dot(mixed.astype(BF16), wo_ref[...])
    h_ref[...] = h
    hn = _rms(h, g2_ref[...])
    hn_ref[...] = hn
    q = _dot(hn.astype(BF16), wq_ref[...])
    n_half = 2 * PEER_HEADS
    dh = q.shape[1] // n_half
    for j in range(n_half):
        q_scr[j] = q[:, j * dh:(j + 1) * dh].astype(BF16)

    def head(hd, carry):
        sv0, si0 = _top16(_dot_nt(keys_ref[2 * hd], q_scr[2 * hd]))
        sv1, si1 = _top16(_dot_nt(keys_ref[2 * hd + 1], q_scr[2 * hd + 1]))
        cand_scr[...] = jnp.full(cand_scr.shape, -jnp.inf, F32)
        cidx_scr[...] = jnp.full(cidx_scr.shape, -1.0, F32)
        off = 0
        for a in range(PEER_TOPK):
            nb = PEER_TOPK // (a + 1)
            cand_scr[off:off + nb, :] = sv0[a:a + 1] + sv1[:nb]
            cidx_scr[off:off + nb, :] = (si0[a:a + 1] * float(PEER_NKEYS) + si1[:nb]) * float(idx_scale)
            off += nb
        best, eid = _top16(cand_scr[...], cidx_scr[...])
        ex = jnp.exp(best - best[0:1])
        r0 = pl.multiple_of(hd * PEER_TOPK, PEER_TOPK)
        idx_scr[pl.ds(r0, PEER_TOPK), :] = eid.astype(I32)
        gate_scr[pl.ds(r0, PEER_TOPK), :] = ex / jnp.sum(ex, axis=0, keepdims=True)
        return carry

    lax.fori_loop(0, PEER_HEADS, head, 0)
    idx_ref[...] = idx_scr[...].T
    gate_ref[...] = gate_scr[...].T


def _route(hyb, p1, sg, x, who_bf, wo_bf, g2, wq_bf, keys_bf, tm, idx_scale):
    t, d = x.shape
    d_h = hyb.shape[1]
    n_half, nk, dh = keys_bf.shape
    n_cand = sum(PEER_TOPK // (a + 1) for a in range(PEER_TOPK))
    n_cand = -(-n_cand // SUBLANES) * SUBLANES
    tok = lambda w: pl.BlockSpec((tm, w), lambda i: (i, 0))
    full = lambda a: pl.BlockSpec(a.shape, lambda i: tuple(0 for _ in a.shape))
    return pl.pallas_call(
        functools.partial(_route_kernel, idx_scale=idx_scale),
        grid=(t // tm,),
        in_specs=[tok(d_h), tok(d), pl.BlockSpec((tm, d), lambda i: (i, 1)), tok(d),
                  full(who_bf), full(wo_bf), pl.BlockSpec((1, d), lambda i: (0, 0)), full(wq_bf), full(keys_bf)],
        out_specs=[tok(d), tok(d), tok(PEER_SLOTS), tok(PEER_SLOTS)],
        out_shape=[jax.ShapeDtypeStruct((t, d), F32), jax.ShapeDtypeStruct((t, d), F32),
                   jax.ShapeDtypeStruct((t, PEER_SLOTS), I32), jax.ShapeDtypeStruct((t, PEER_SLOTS), F32)],
        scratch_shapes=[pltpu.VMEM((n_half, tm, dh), BF16),
                        pltpu.VMEM((PEER_SLOTS, tm), I32),
                        pltpu.VMEM((PEER_SLOTS, tm), F32),
                        pltpu.VMEM((n_cand, tm), F32),
                        pltpu.VMEM((n_cand, tm), F32)],
        compiler_params=_params(("parallel",), 56),
        name="route",
    )(hyb, p1, sg, x, who_bf, wo_bf, g2.reshape(1, d), wq_bf, keys_bf)


def _gather_rows(tab_ref, idx_ref, j, raw_ref, dst_row):
    rows = raw_ref.shape[0] // PEER_SLOTS
    for e in range(PEER_SLOTS):
        r = pl.multiple_of(idx_ref[j, e], rows)
        d = dst_row(e)
        raw_ref[d:d + rows, :] = tab_ref[pl.ds(r, rows), :]


def _split2(x):
    hi = x.astype(BF16)
    return hi, (x - hi.astype(F32)).astype(BF16)


def _gelu(x):
    return 0.5 * x * (1.0 + jnp.tanh(math.sqrt(2.0 / math.pi) * (x + 0.044715 * (x * x * x))))


ID_CHUNK = 16
N_GATHER_BUFS = 2


def _id_copy(idx_hbm, chunk, dst, sem):
    return pltpu.make_async_copy(idx_hbm.at[pl.ds(chunk * ID_CHUNK, ID_CHUNK), :], dst, sem)


def _token_pipeline(idx_hbm, id_bufs, sems, tb, gather, compute, raws):
    chunks_per_step = tb // ID_CHUNK
    assert chunks_per_step % 2 == 0
    first = pl.program_id(0) * chunks_per_step
    total = pl.num_programs(0) * chunks_per_step

    def region(ids, t_base):
        gather(ids, 0, raws[0])
        for j in range(ID_CHUNK):
            if j + 1 < ID_CHUNK:
                gather(ids, j + 1, raws[(j + 1) % len(raws)])
            compute(t_base + j, raws[j % len(raws)])

    def two_chunks(k, carry):
        c0 = first + 2 * k

        @pl.when(c0 == 0)
        def _():
            _id_copy(idx_hbm, c0, id_bufs[0], sems.at[0]).start()

        _id_copy(idx_hbm, c0 + 1, id_bufs[1], sems.at[1]).start()
        _id_copy(idx_hbm, c0, id_bufs[0], sems.at[0]).wait()
        region(id_bufs[0], 2 * k * ID_CHUNK)

        @pl.when(c0 + 2 < total)
        def _():
            _id_copy(idx_hbm, c0 + 2, id_bufs[0], sems.at[0]).start()

        _id_copy(idx_hbm, c0 + 1, id_bufs[1], sems.at[1]).wait()
        region(id_bufs[1], (2 * k + 1) * ID_CHUNK)
        return carry

    lax.fori_loop(0, chunks_per_step // 2, two_chunks, 0)


def _id_scratch():
    return [pltpu.SMEM((ID_CHUNK, PEER_SLOTS), I32), pltpu.SMEM((ID_CHUNK, PEER_SLOTS), I32),
            pltpu.SemaphoreType.DMA((2,))]


U_GROUP_ORDER = (0, 2, 1, 3)


def _u_raw_row(e):
    return ((e // SUBLANES) * 4 + U_GROUP_ORDER[e % 4]) * SUBLANES + ((e % SUBLANES) // 4) * 4


def _peer_u_kernel(idx_hbm, x_ref, gate_ref, rep_ref, tab_ref, arep_ref, pre_ref, id_a, id_b, sems, *raws):
    tb = x_ref.shape[0]
    rows = raws[0].shape[0] // PEER_SLOTS
    assert rows == 4, "the sublane butterfly below is written for 4 packed rows per expert"
    ones = jnp.ones((SUBLANES, LANES), BF16)
    sub = lax.broadcasted_iota(I32, (SUBLANES, LANES), 0)
    low_pair = (sub % 4) < 2
    even_row = (sub % 2) == 0

    def gather(ids, j, raw_ref):
        _gather_rows(tab_ref, ids, j, raw_ref, _u_raw_row)

    def compute(t, raw_ref):
        x_lo = jnp.concatenate([x_ref[t, pl.ds(0, rows, stride=2), :]] * 2, axis=0)
        x_hi = jnp.concatenate([x_ref[t, pl.ds(1, rows, stride=2), :]] * 2, axis=0)
        merged = []
        for m in range(PEER_SLOTS // SUBLANES):
            a = []
            for j in range(4):
                k = 4 * m + j
                w = raw_ref[k * SUBLANES:(k + 1) * SUBLANES, :]
                p = pltpu.bitcast(w << 16, F32) * x_lo + pltpu.bitcast(w & HI_MASK, F32) * x_hi
                a.append(p + pltpu.roll(p, SUBLANES - 2, axis=0))
            ab = jnp.where(low_pair, a[0], pltpu.roll(a[1], 2, axis=0))
            cd = jnp.where(low_pair, a[2], pltpu.roll(a[3], 2, axis=0))
            ab = ab + pltpu.roll(ab, SUBLANES - 1, axis=0)
            cd = cd + pltpu.roll(cd, SUBLANES - 1, axis=0)
            merged.append(jnp.where(even_row, ab, pltpu.roll(cd, 1, axis=0)))
        p_hi, p_lo = _split2(jnp.concatenate(merged, axis=0))
        sums = _dot_nt(ones, p_hi) + _dot_nt(ones, p_lo)
        pre_ref[pl.ds(t, 1), :] = sums[0:1]

    _token_pipeline(idx_hbm, (id_a, id_b), sems, tb, gather, compute, raws)
    act = _gelu(pre_ref[...]) * gate_ref[...]
    a1 = act.astype(BF16)
    r1 = act - a1.astype(F32)
    a2, a3 = _split2(r1)
    rep = rep_ref[...]
    arep_ref[...] = _dot(a1, rep) + _dot(a2, rep) + _dot(a3, rep)


def _peer_u(idx, x3, gate, tab, tb):
    t, nch, _ = x3.shape
    rows = tab.shape[0] // (PEER_NKEYS * PEER_NKEYS)
    rep = np.zeros((PEER_SLOTS, PEER_SLOTS * nch), np.float32)
    for e in range(PEER_SLOTS):
        rep[e, e * nch:(e + 1) * nch] = 1.0
    return pl.pallas_call(
        _peer_u_kernel,
        grid=(t // tb,),
        in_specs=[pl.BlockSpec(memory_space=pl.ANY),
                  pl.BlockSpec((tb, nch, LANES), lambda i: (i, 0, 0)),
                  pl.BlockSpec((tb, PEER_SLOTS), lambda i: (i, 0)),
                  pl.BlockSpec(rep.shape, lambda i: (0, 0)),
                  pl.BlockSpec(tab.shape, lambda i: (0, 0), pipeline_mode=pl.Buffered(1))],
        out_specs=pl.BlockSpec((tb, PEER_SLOTS * nch), lambda i: (i, 0)),
        out_shape=jax.ShapeDtypeStruct((t, PEER_SLOTS * nch), F32),
        scratch_shapes=([pltpu.VMEM((tb, PEER_SLOTS), F32)] + _id_scratch()
                        + [pltpu.VMEM((PEER_SLOTS * rows, LANES), I32)] * N_GATHER_BUFS),
        compiler_params=_params(("arbitrary",), 48),
        name="peer_u",
    )(idx, x3, gate, jnp.asarray(rep, BF16), tab)


def _peer_v_kernel(idx_hbm, arep_ref, h_ref, g_ref, tab_ref, y_ref, id_a, id_b, sems, *raws):
    tb, nch, _ = h_ref.shape
    rows = raws[0].shape[0] // PEER_SLOTS
    shape = (nch, arep_ref.shape[1])
    own_chunk = (lax.broadcasted_iota(I32, shape, 1) % nch) == lax.broadcasted_iota(I32, shape, 0)

    def gather(ids, j, raw_ref):
        _gather_rows(tab_ref, ids, j, raw_ref, lambda e: e * rows)

    def compute(t, raw_ref):
        vals = pltpu.bitcast(raw_ref[...], BF16)
        w_hi, w_lo = _split2(jnp.where(own_chunk, arep_ref[pl.ds(t, 1), :], 0.0))
        y_ref[t] = _dot(w_hi, vals) + _dot(w_lo, vals)

    _token_pipeline(idx_hbm, (id_a, id_b), sems, tb, gather, compute, raws)
    y = h_ref[...] + y_ref[...]
    ms = jnp.sum(jnp.sum(y * y, axis=2, keepdims=True), axis=1, keepdims=True) * (1.0 / (y.shape[1] * y.shape[2]))
    y_ref[...] = y * lax.rsqrt(ms + EPS) * g_ref[...]


def _peer_v(idx, arep, h3, g3, tab, tb):
    t = h3.shape[0]
    rows = tab.shape[0] // (PEER_NKEYS * PEER_NKEYS)
    blk = pl.BlockSpec((tb,) + h3.shape[1:], lambda i: (i, 0, 0))
    return pl.pallas_call(
        _peer_v_kernel,
        grid=(t // tb,),
        in_specs=[pl.BlockSpec(memory_space=pl.ANY),
                  pl.BlockSpec((tb, arep.shape[1]), lambda i: (i, 0)),
                  blk,
                  pl.BlockSpec(g3.shape, lambda i: (0, 0, 0)),
                  pl.BlockSpec(tab.shape, lambda i: (0, 0), pipeline_mode=pl.Buffered(1))],
        out_specs=blk,
        out_shape=jax.ShapeDtypeStruct(h3.shape, F32),
        scratch_shapes=_id_scratch() + [pltpu.VMEM((PEER_SLOTS * rows, LANES), I32)] * N_GATHER_BUFS,
        compiler_params=_params(("arbitrary",), 48),
        name="peer_v",
    )(idx, arep, h3, g3, tab)


def _pick_tile(n, want):
    while n % want:
        want //= 2
    return want


def kernel(x_prompt, x_sample, norm1_g, w_in, pool_w, pool_scale, conv_w, conv_b, filt_w1, filt_b1, filt_f1, filt_w2, filt_b2, filt_f2, filt_w3, hyena_bias, w_pool_out, w_hyena_out, w_o, norm2_g, peer_wq, peer_keys, peer_u, peer_v, normf_g):
    assert norm1_g.shape[0] == 1, "single-layer block"
    assert x_prompt.shape[1:] == x_sample.shape[1:]
    nb_p, seq_len, d = x_prompt.shape
    d_pool = pool_scale.shape[-1]
    d_hy = conv_b.shape[-1]
    d_h = d_hy // 3
    x = jnp.concatenate([x_prompt, x_sample], axis=0).reshape(-1, d)
    t = x.shape[0]
    tm = _pick_tile(seq_len, 512)

    ahy, sg = _inproj(x, norm1_g[0], w_in[0].astype(BF16), d_pool + d_hy, tm)
    p1, x0, z, zb = _local(ahy, sg, pool_w[0].astype(BF16), pool_scale[0], conv_w[0], conv_b[0],
                           w_pool_out[0].astype(BF16), seq_len, d_pool, d_hy, tm)
    ks, kd = _filters(filt_w1[0], filt_b1[0], filt_f1[0], filt_w2[0], filt_b2[0], filt_f2[0], filt_w3[0],
                      seq_len, d_h)
    cmat, smat, stmat = _dftgen(seq_len)
    kr, ki = _spectrum(cmat, smat, ks, kd, tm)
    hyb = _long_conv(cmat, smat, stmat, zb, z, x0, kr, ki, hyena_bias[0], seq_len, tm)

    keys = peer_keys[0]
    keys_bf = keys.reshape((-1,) + keys.shape[2:]).astype(BF16)
    h, hn, idx, gate = _route(hyb, p1, sg, x, w_hyena_out[0].astype(BF16), w_o[0].astype(BF16), norm2_g[0],
                              peer_wq[0].astype(BF16), keys_bf, _pick_tile(seq_len, 256),
                              idx_scale=d // (2 * LANES))

    tok_shape = (t, d // LANES, LANES)
    tb = _pick_tile(seq_len, 256)
    arep = _peer_u(idx, hn.reshape(tok_shape), gate, _pack_table(peer_u[0]), tb)
    y = _peer_v(idx, arep, h.reshape(tok_shape), normf_g.reshape((1,) + tok_shape[1:]), _pack_table(peer_v[0]), tb)
    y = y.reshape(-1, seq_len, d)
    return (y[:nb_p], y[nb_p:])
```

```python
import functools
import math

import jax
import jax.numpy as jnp
import numpy as np
from jax import lax
from jax.experimental import pallas as pl
from jax.experimental.pallas import tpu as pltpu

F32 = jnp.float32
BF16 = jnp.bfloat16
I32 = jnp.int32

EPS = 1e-6
POOL_WINDOWS = (2, 4, 8, 16)
POOL_HALO = 8
PEER_HEADS = 8
PEER_NKEYS = 128
PEER_TOPK = 16
PEER_SLOTS = PEER_HEADS * PEER_TOPK
FILTER_EMB = 33
DECAY_TARGET = 1e-2
FAST_DECAY_PCT = 0.3
SLOW_DECAY_PCT = 1.5
LANES = 128
SUBLANES = 8
HI_MASK = -65536
MIB = 1024 * 1024


def _params(sem, vmem_mib):
    return pltpu.CompilerParams(dimension_semantics=sem, vmem_limit_bytes=vmem_mib * MIB)


def _dot(a, b):
    return jnp.dot(a, b, preferred_element_type=F32)


def _dot_nt(a, b):
    return lax.dot_general(a, b, (((1,), (1,)), ((), ())), preferred_element_type=F32)


def _dot_f32(a, b):
    return jnp.dot(a, b, preferred_element_type=F32, precision=lax.Precision.HIGHEST)


def _rms(x, g):
    return x * lax.rsqrt(jnp.mean(x * x, axis=-1, keepdims=True) + EPS) * g


def _pack_kernel(t_ref, o_ref):
    x = t_ref[...]
    for r in range(x.shape[1] // (2 * LANES)):
        lo = pltpu.bitcast(x[:, (2 * r) * LANES:(2 * r + 1) * LANES].astype(BF16).astype(F32), I32)
        hi = pltpu.bitcast(x[:, (2 * r + 1) * LANES:(2 * r + 2) * LANES].astype(BF16).astype(F32), I32)
        o_ref[:, r * LANES:(r + 1) * LANES] = hi | lax.shift_right_logical(lo, 16)


def _pack_table(tab):
    e, d = tab.shape
    rows = 512
    packed = pl.pallas_call(
        _pack_kernel,
        grid=(e // rows,),
        in_specs=[pl.BlockSpec((rows, d), lambda i: (i, 0))],
        out_specs=pl.BlockSpec((rows, d // 2), lambda i: (i, 0)),
        out_shape=jax.ShapeDtypeStruct((e, d // 2), I32),
        compiler_params=_params(("parallel",), 32),
        name="pack_table",
    )(tab)
    return packed.reshape(e * (d // 2) // LANES, LANES)


def _two_part_specs(tm, d, n_first):
    return [pl.BlockSpec((tm, d), lambda i: (jnp.minimum(i, n_first - 1), 0)),
            pl.BlockSpec((tm, d), lambda i: (jnp.maximum(i - n_first, 0), 0))]


def _two_part_block(x1_ref, x2_ref, n_first):
    return jnp.where(pl.program_id(0) < n_first, x1_ref[...], x2_ref[...])


def _inproj_kernel(x1_ref, x2_ref, g_ref, w_ref, ahy_ref, sg_ref, *, n_ahy, chunk, n_first):
    xn = _rms(_two_part_block(x1_ref, x2_ref, n_first), g_ref[...]).astype(BF16)
    for j in range(n_ahy // chunk):
        ahy_ref[:, j * chunk:(j + 1) * chunk] = _dot(xn, w_ref[:, j * chunk:(j + 1) * chunk])
    n_g = w_ref.shape[1] - n_ahy
    for j in range(n_g // chunk):
        g = _dot(xn, w_ref[:, n_ahy + j * chunk:n_ahy + (j + 1) * chunk])
        sg_ref[:, j * chunk:(j + 1) * chunk] = 1.0 / (1.0 + jnp.exp(-g))


def _inproj(x1, x2, g, w_bf, n_ahy, tm):
    d = x1.shape[1]
    t = x1.shape[0] + x2.shape[0]
    n = w_bf.shape[1]
    n_first = x1.shape[0] // tm
    return pl.pallas_call(
        functools.partial(_inproj_kernel, n_ahy=n_ahy, chunk=min(1024, n_ahy, n - n_ahy), n_first=n_first),
        grid=(t // tm,),
        in_specs=_two_part_specs(tm, d, n_first) + [
                  pl.BlockSpec((1, d), lambda i: (0, 0)),
                  pl.BlockSpec((d, n), lambda i: (0, 0))],
        out_specs=[pl.BlockSpec((tm, n_ahy), lambda i: (i, 0)),
                   pl.BlockSpec((tm, n - n_ahy), lambda i: (i, 0))],
        out_shape=[jax.ShapeDtypeStruct((t, n_ahy), F32), jax.ShapeDtypeStruct((t, n - n_ahy), F32)],
        compiler_params=_params(("parallel",), 56),
        name="inproj",
    )(x1, x2, g.reshape(1, d), w_bf)


def _local_kernel(ahy_ref, prev_ref, next_ref, sga_ref, pw_ref, ps_ref, cw_ref, cb_ref, wpo_ref,
                  p1_ref, x0_ref, z_ref, zb_ref, *, seq_len, d_pool, d_hy):
    tm = ahy_ref.shape[0]
    ext_rows = tm + 2 * POOL_HALO
    tiles_per_seq = seq_len // tm
    j = pl.program_id(0) % tiles_per_seq
    prev = jnp.where(j == 0, 0.0, prev_ref[...])
    nxt = jnp.where(j == tiles_per_seq - 1, 0.0, next_ref[...])
    pos = j * tm + lax.broadcasted_iota(I32, (tm, 1), 0)

    def ext(lo, hi):
        return jnp.concatenate([prev[:, lo:hi], ahy_ref[:, lo:hi], nxt[:, lo:hi]], axis=0)

    def shifted(v, k):
        return pltpu.roll(v, k % ext_rows, axis=0)

    group = d_pool // len(POOL_WINDOWS)
    mixed = []
    for g, w in enumerate(POOL_WINDOWS):
        half = w // 2
        e = ext(g * group, (g + 1) * group)
        win = e + shifted(e, 1)
        span = 1
        while span < half:
            win = shifted(win, span) + shifted(win, -span)
            span *= 2
        win = win[POOL_HALO:POOL_HALO + tm]
        cnt = (jnp.minimum(pos + half, seq_len) - jnp.maximum(pos - half, 0)).astype(F32)
        pooled = win / cnt - ahy_ref[:, g * group:(g + 1) * group]
        mixed.append(_dot(pooled.astype(BF16), pw_ref[g]))
    pm = jnp.concatenate(mixed, axis=1) * ps_ref[...]
    p1_ref[...] = sga_ref[...] * _dot(pm.astype(BF16), wpo_ref[...])

    d_br = d_hy // 3
    branches = []
    for b in range(3):
        lo, hi = d_pool + b * d_br, d_pool + (b + 1) * d_br
        e = ext(lo, hi)
        c0, c1 = b * d_br, (b + 1) * d_br
        uc = (shifted(e, 1)[POOL_HALO:POOL_HALO + tm] * cw_ref[0:1, c0:c1]
              + ahy_ref[:, lo:hi] * cw_ref[1:2, c0:c1]
              + shifted(e, -1)[POOL_HALO:POOL_HALO + tm] * cw_ref[2:3, c0:c1]
              + cb_ref[:, c0:c1])
        branches.append(uc)
    x0, x1, v = branches
    z = v * x1
    x0_ref[...] = x0
    z_ref[...] = z
    zb_ref[...] = z.astype(BF16)


def _local(ahy, sg, pool_w_bf, pool_scale, conv_w, conv_b, w_pool_out_bf, seq_len, d_pool, d_hy, tm):
    t, n_ahy = ahy.shape
    d_model = w_pool_out_bf.shape[1]
    d_br = d_hy // 3
    hb = tm // POOL_HALO
    n_halo_blocks = t // POOL_HALO
    return pl.pallas_call(
        functools.partial(_local_kernel, seq_len=seq_len, d_pool=d_pool, d_hy=d_hy),
        grid=(t // tm,),
        in_specs=[pl.BlockSpec((tm, n_ahy), lambda i: (i, 0)),
                  pl.BlockSpec((POOL_HALO, n_ahy), lambda i: (jnp.maximum(i * hb - 1, 0), 0)),
                  pl.BlockSpec((POOL_HALO, n_ahy), lambda i: (jnp.minimum((i + 1) * hb, n_halo_blocks - 1), 0)),
                  pl.BlockSpec((tm, d_model), lambda i: (i, 0)),
                  pl.BlockSpec(pool_w_bf.shape, lambda i: (0, 0, 0)),
                  pl.BlockSpec((1, d_pool), lambda i: (0, 0)),
                  pl.BlockSpec(conv_w.shape, lambda i: (0, 0)),
                  pl.BlockSpec((1, d_hy), lambda i: (0, 0)),
                  pl.BlockSpec(w_pool_out_bf.shape, lambda i: (0, 0))],
        out_specs=[pl.BlockSpec((tm, d_model), lambda i: (i, 0)),
                   pl.BlockSpec((tm, d_br), lambda i: (i, 0)),
                   pl.BlockSpec((tm, d_br), lambda i: (i, 0)),
                   pl.BlockSpec((tm, d_br), lambda i: (i, 0))],
        out_shape=[jax.ShapeDtypeStruct((t, d_model), F32),
                   jax.ShapeDtypeStruct((t, d_br), F32),
                   jax.ShapeDtypeStruct((t, d_br), F32),
                   jax.ShapeDtypeStruct((t, d_br), BF16)],
        compiler_params=_params(("parallel",), 56),
        name="local_mix",
    )(ahy, ahy, ahy, sg, pool_w_bf, pool_scale.reshape(1, d_pool), conv_w, conv_b.reshape(1, d_hy), w_pool_out_bf)


def _filter_kernel(freq_ref, w1_ref, b1_ref, f1_ref, w2_ref, b2_ref, f2_ref, w3_ref, dl_ref, ks_ref, kd_ref,
                   *, seq_len):
    tl = ks_ref.shape[0]
    d_h = ks_ref.shape[1]
    i = (pl.program_id(0) * tl + lax.broadcasted_iota(I32, (tl, 1), 0)).astype(F32)
    t = i / (seq_len - 1.0)
    wpos = (2.0 * math.pi) * i / seq_len
    lane = lax.broadcasted_iota(I32, (tl, LANES), 1)
    bands = (FILTER_EMB - 1) // 2
    ang = freq_ref[...] * wpos
    feat = jnp.where(lane == 0, t,
                     jnp.where(lane <= bands, jnp.cos(ang),
                               jnp.where(lane <= 2 * bands, -jnp.sin(ang), 0.0)))
    h = jnp.sin(f1_ref[...] * (_dot_f32(feat, w1_ref[...]) + b1_ref[...]))
    h = jnp.sin(f2_ref[...] * (_dot_f32(h, w2_ref[...]) + b2_ref[...]))
    h = _dot_f32(h, w3_ref[...])
    decay = jnp.exp(-t * jnp.abs(dl_ref[...]))
    kf = h[:, :d_h] * decay
    kb = h[:, d_h:] * decay
    ks_ref[...] = kf + kb
    kd_ref[...] = kf - kb


def _filters(w1, b1, f1, w2, b2, f2, w3, seq_len, d_h):
    hid = w1.shape[1]
    bands = (FILTER_EMB - 1) // 2
    freqs = np.zeros((1, LANES), np.float32)
    fr = np.linspace(1e-4, bands - 1, bands, dtype=np.float32)
    freqs[0, 1:1 + bands] = fr
    freqs[0, 1 + bands:1 + 2 * bands] = fr
    w1p = jnp.zeros((LANES, hid), F32).at[:FILTER_EMB].set(w1)
    max_decay = math.log(DECAY_TARGET) / FAST_DECAY_PCT
    min_decay = math.log(DECAY_TARGET) / SLOW_DECAY_PCT
    deltas = np.linspace(min_decay, max_decay, d_h, dtype=np.float32).reshape(1, d_h)
    tl = min(512, seq_len)
    full = lambda shape: pl.BlockSpec(shape, lambda i: tuple(0 for _ in shape))
    return pl.pallas_call(
        functools.partial(_filter_kernel, seq_len=seq_len),
        grid=(seq_len // tl,),
        in_specs=[full((1, LANES)), full((LANES, hid)), full((1, hid)), full((1, hid)),
                  full((hid, hid)), full((1, hid)), full((1, hid)), full((hid, 2 * d_h)), full((1, d_h))],
        out_specs=[pl.BlockSpec((tl, d_h), lambda i: (i, 0)), pl.BlockSpec((tl, d_h), lambda i: (i, 0))],
        out_shape=[jax.ShapeDtypeStruct((seq_len, d_h), F32), jax.ShapeDtypeStruct((seq_len, d_h), F32)],
        compiler_params=_params(("parallel",), 32),
        name="hyena_filters",
    )(jnp.asarray(freqs), w1p, b1.reshape(1, hid), f1.reshape(1, hid), w2, b2.reshape(1, hid),
      f2.reshape(1, hid), w3, jnp.asarray(deltas))


def _dftgen_kernel(c_ref, s_ref, st_ref, *, n_fft):
    tr, n = c_ref.shape
    row = pl.program_id(0) * tr + lax.broadcasted_iota(I32, (tr, n), 0)
    col = lax.broadcasted_iota(I32, (tr, n), 1)
    ang = ((row * col) & (n_fft - 1)).astype(F32) * (2.0 * math.pi / n_fft)
    c_ref[...] = jnp.cos(ang).astype(BF16)
    ms = -jnp.sin(ang)
    sign_col = (1 - 2 * (col & 1)).astype(F32)
    sign_row = (1 - 2 * (row & 1)).astype(F32)
    s_ref[...] = jnp.where(row == 0, sign_col, ms).astype(BF16)
    st_ref[...] = jnp.where(col == 0, sign_row, ms).astype(BF16)


def _dftgen(seq_len):
    tr = min(256, seq_len)
    spec = pl.BlockSpec((tr, seq_len), lambda i: (i, 0))
    shp = jax.ShapeDtypeStruct((seq_len, seq_len), BF16)
    return pl.pallas_call(
        functools.partial(_dftgen_kernel, n_fft=2 * seq_len),
        grid=(seq_len // tr,),
        out_specs=[spec, spec, spec],
        out_shape=[shp, shp, shp],
        compiler_params=_params(("parallel",), 48),
        name="dft_matrices",
    )()


def _spectrum_kernel(c_ref, s_ref, ks_ref, kd_ref, kr_ref, ki_ref):
    ks = ks_ref[...]
    kr_ref[...] = _dot(c_ref[...], ks.astype(BF16))
    ki = _dot(s_ref[...], kd_ref[...].astype(BF16))
    tf = ki.shape[0]
    row = lax.broadcasted_iota(I32, (tf, 1), 0)
    sign = (1 - 2 * (lax.broadcasted_iota(I32, (ks.shape[0], 1), 0) & 1)).astype(F32)
    nyq = jnp.sum(ks * sign, axis=0, keepdims=True)
    first = jnp.logical_and(pl.program_id(0) == 0, row == 0)
    ki_ref[...] = jnp.where(first, nyq, ki)


def _spectrum(cmat, smat, ks, kd, tf):
    n, d_h = ks.shape
    return pl.pallas_call(
        _spectrum_kernel,
        grid=(n // tf,),
        in_specs=[pl.BlockSpec((tf, n), lambda i: (i, 0)), pl.BlockSpec((tf, n), lambda i: (i, 0)),
                  pl.BlockSpec((n, d_h), lambda i: (0, 0)), pl.BlockSpec((n, d_h), lambda i: (0, 0))],
        out_specs=[pl.BlockSpec((tf, d_h), lambda i: (i, 0)), pl.BlockSpec((tf, d_h), lambda i: (i, 0))],
        out_shape=[jax.ShapeDtypeStruct((n, d_h), F32), jax.ShapeDtypeStruct((n, d_h), F32)],
        compiler_params=_params(("parallel",), 56),
        name="filter_spectrum",
    )(cmat, smat, ks, kd)


def _fwd_kernel(c_ref, s_ref, z_ref, kr_ref, ki_ref, yr_ref, yi_ref, *, n_fft):
    z = z_ref[...]
    zr = _dot(c_ref[...], z)
    zi = _dot(s_ref[...], z)
    kr = kr_ref[...]
    ki = ki_ref[...]
    row = lax.broadcasted_iota(I32, (zr.shape[0], 1), 0)
    first = jnp.logical_and(pl.program_id(0) == 0, row == 0)
    yr = jnp.where(first, zr * kr, zr * kr - zi * ki)
    yi = jnp.where(first, zi * ki, zr * ki + zi * kr)
    scale = jnp.where(first, 1.0 / n_fft, 2.0 / n_fft)
    yr_ref[...] = (yr * scale).astype(BF16)
    yi_ref[...] = (yi * scale).astype(BF16)


def _inv_kernel(c_ref, st_ref, yr_ref, yi_ref, z_ref, x0_ref, bias_ref, o_ref):
    y = _dot(c_ref[...], yr_ref[...]) + _dot(st_ref[...], yi_ref[...])
    o_ref[...] = ((y + z_ref[...] * bias_ref[...]) * x0_ref[...]).astype(BF16)


def _long_conv(cmat, smat, stmat, zb, z, x0, kr, ki, bias, seq_len, tf):
    t, d_h = z.shape
    nb = t // seq_len
    nf = seq_len // tf
    mat_spec = pl.BlockSpec((tf, seq_len), lambda f, b: (f, 0))
    seq_spec = pl.BlockSpec((seq_len, d_h), lambda f, b: (b, 0))
    tile_spec = pl.BlockSpec((tf, d_h), lambda f, b: (b * nf + f, 0))
    k_spec = pl.BlockSpec((tf, d_h), lambda f, b: (f, 0))
    yr, yi = pl.pallas_call(
        functools.partial(_fwd_kernel, n_fft=2 * seq_len),
        grid=(nf, nb),
        in_specs=[mat_spec, mat_spec, seq_spec, k_spec, k_spec],
        out_specs=[tile_spec, tile_spec],
        out_shape=[jax.ShapeDtypeStruct((t, d_h), BF16), jax.ShapeDtypeStruct((t, d_h), BF16)],
        compiler_params=_params(("parallel", "parallel"), 56),
        name="conv_fwd_dft",
    )(cmat, smat, zb, kr, ki)
    return pl.pallas_call(
        _inv_kernel,
        grid=(nf, nb),
        in_specs=[mat_spec, mat_spec, seq_spec, seq_spec, tile_spec, tile_spec,
                  pl.BlockSpec((1, d_h), lambda f, b: (0, 0))],
        out_specs=tile_spec,
        out_shape=jax.ShapeDtypeStruct((t, d_h), BF16),
        compiler_params=_params(("parallel", "parallel"), 56),
        name="conv_inv_dft",
    )(cmat, stmat, yr, yi, z, x0, bias.reshape(1, d_h))


def _top16(s, payload=None):
    n_rows, n = s.shape
    groups = n_rows // SUBLANES
    slabs = [s[g * SUBLANES:(g + 1) * SUBLANES] for g in range(groups)]
    sub = lax.broadcasted_iota(I32, (SUBLANES, n), 0).astype(F32)
    rid = [sub + float(g * SUBLANES) for g in range(groups)]
    vals, ids = [], []
    for _ in range(PEER_TOPK):
        m = jnp.max(functools.reduce(jnp.maximum, slabs), axis=0, keepdims=True)
        first = jnp.full((SUBLANES, n), float(groups), F32)
        for g in reversed(range(groups)):
            first = jnp.where(slabs[g] == m, float(g), first)
        row = jnp.where(first < float(groups), first * float(SUBLANES) + sub, float(n_rows))
        pos = jnp.min(row, axis=0, keepdims=True)
        hits = [r == pos for r in rid]
        vals.append(m)
        if payload is None:
            ids.append(pos)
        else:
            picked = [jnp.where(hits[g], payload[g * SUBLANES:(g + 1) * SUBLANES], -1.0) for g in range(groups)]
            ids.append(jnp.max(functools.reduce(jnp.maximum, picked), axis=0, keepdims=True))
        slabs = [jnp.where(h, -jnp.inf, sl) for h, sl in zip(hits, slabs)]
    return jnp.concatenate(vals, axis=0), jnp.concatenate(ids, axis=0)


def _route_kernel(hyb_ref, p1_ref, sgb_ref, x1_ref, x2_ref, who_ref, wo_ref, g2_ref, wq_ref, keys_ref,
                  h_ref, hn_ref, idx_ref, gate_ref, q_scr, idx_scr, gate_scr, cand_scr, cidx_scr,
                  *, idx_scale, n_first):
    yb = _dot(hyb_ref[...], who_ref[...])
    mixed = p1_ref[...] + sgb_ref[...] * yb
    h = _two_part_block(x1_ref, x2_ref, n_first) + _dot(mixed.astype(BF16), wo_ref[...])
    hn = _rms(h, g2_ref[...])
    for c in range(h.shape[1] // LANES):
        h_ref[:, c, :] = h[:, c * LANES:(c + 1) * LANES]
        hn_ref[:, c, :] = hn[:, c * LANES:(c + 1) * LANES]
    q = _dot(hn.astype(BF16), wq_ref[...])
    n_half = 2 * PEER_HEADS
    dh = q.shape[1] // n_half
    for j in range(n_half):
        q_scr[j] = q[:, j * dh:(j + 1) * dh].astype(BF16)

    def head(hd, carry):
        sv0, si0 = _top16(_dot_nt(keys_ref[2 * hd], q_scr[2 * hd]))
        sv1, si1 = _top16(_dot_nt(keys_ref[2 * hd + 1], q_scr[2 * hd + 1]))
        cand_scr[...] = jnp.full(cand_scr.shape, -jnp.inf, F32)
        cidx_scr[...] = jnp.full(cidx_scr.shape, -1.0, F32)
        off = 0
        for a in range(PEER_TOPK):
            nb = PEER_TOPK // (a + 1)
            cand_scr[off:off + nb, :] = sv0[a:a + 1] + sv1[:nb]
            cidx_scr[off:off + nb, :] = (si0[a:a + 1] * float(PEER_NKEYS) + si1[:nb]) * float(idx_scale)
            off += nb
        best, eid = _top16(cand_scr[...], cidx_scr[...])
        ex = jnp.exp(best - best[0:1])
        r0 = pl.multiple_of(hd * PEER_TOPK, PEER_TOPK)
        idx_scr[pl.ds(r0, PEER_TOPK), :] = eid.astype(I32)
        gate_scr[pl.ds(r0, PEER_TOPK), :] = ex / jnp.sum(ex, axis=0, keepdims=True)
        return carry

    lax.fori_loop(0, PEER_HEADS, head, 0)
    idx_ref[...] = idx_scr[...].T
    gate_ref[...] = gate_scr[...].T


def _route(hyb, p1, sg, x1, x2, who_bf, wo_bf, g2, wq_bf, keys_bf, tm, idx_scale):
    d = x1.shape[1]
    t = x1.shape[0] + x2.shape[0]
    n_first = x1.shape[0] // tm
    d_h = hyb.shape[1]
    n_half, nk, dh = keys_bf.shape
    n_cand = sum(PEER_TOPK // (a + 1) for a in range(PEER_TOPK))
    n_cand = -(-n_cand // SUBLANES) * SUBLANES
    tok = lambda w: pl.BlockSpec((tm, w), lambda i: (i, 0))
    full = lambda a: pl.BlockSpec(a.shape, lambda i: tuple(0 for _ in a.shape))
    tok3 = pl.BlockSpec((tm, d // LANES, LANES), lambda i: (i, 0, 0))
    return pl.pallas_call(
        functools.partial(_route_kernel, idx_scale=idx_scale, n_first=n_first),
        grid=(t // tm,),
        in_specs=[tok(d_h), tok(d), pl.BlockSpec((tm, d), lambda i: (i, 1))] + _two_part_specs(tm, d, n_first) + [
                  full(who_bf), full(wo_bf), pl.BlockSpec((1, d), lambda i: (0, 0)), full(wq_bf), full(keys_bf)],
        out_specs=[tok3, tok3, tok(PEER_SLOTS), tok(PEER_SLOTS)],
        out_shape=[jax.ShapeDtypeStruct((t, d // LANES, LANES), F32), jax.ShapeDtypeStruct((t, d // LANES, LANES), F32),
                   jax.ShapeDtypeStruct((t, PEER_SLOTS), I32), jax.ShapeDtypeStruct((t, PEER_SLOTS), F32)],
        scratch_shapes=[pltpu.VMEM((n_half, tm, dh), BF16),
                        pltpu.VMEM((PEER_SLOTS, tm), I32),
                        pltpu.VMEM((PEER_SLOTS, tm), F32),
                        pltpu.VMEM((n_cand, tm), F32),
                        pltpu.VMEM((n_cand, tm), F32)],
        compiler_params=_params(("parallel",), 56),
        name="route",
    )(hyb, p1, sg, x1, x2, who_bf, wo_bf, g2.reshape(1, d), wq_bf, keys_bf)


def _gather_rows(tab_ref, idx_ref, j, raw_ref, dst_row):
    rows = raw_ref.shape[0] // PEER_SLOTS
    for e in range(PEER_SLOTS):
        r = pl.multiple_of(idx_ref[j, e], rows)
        d = dst_row(e)
        raw_ref[d:d + rows, :] = tab_ref[pl.ds(r, rows), :]


def _split2(x):
    hi = x.astype(BF16)
    return hi, (x - hi.astype(F32)).astype(BF16)


def _gelu(x):
    return 0.5 * x * (1.0 + jnp.tanh(math.sqrt(2.0 / math.pi) * (x + 0.044715 * (x * x * x))))


ID_CHUNK = 32
N_GATHER_BUFS = 2


def _id_copy(idx_hbm, chunk, dst, sem):
    return pltpu.make_async_copy(idx_hbm.at[pl.ds(chunk * ID_CHUNK, ID_CHUNK), :], dst, sem)


def _token_pipeline(idx_hbm, id_bufs, sems, tb, region):
    chunks_per_step = tb // ID_CHUNK
    assert chunks_per_step % 2 == 0
    first = pl.program_id(0) * chunks_per_step
    total = pl.num_programs(0) * chunks_per_step

    def two_chunks(k, carry):
        c0 = first + 2 * k

        @pl.when(c0 == 0)
        def _():
            _id_copy(idx_hbm, c0, id_bufs[0], sems.at[0]).start()

        _id_copy(idx_hbm, c0 + 1, id_bufs[1], sems.at[1]).start()
        _id_copy(idx_hbm, c0, id_bufs[0], sems.at[0]).wait()
        region(id_bufs[0], 2 * k * ID_CHUNK)

        @pl.when(c0 + 2 < total)
        def _():
            _id_copy(idx_hbm, c0 + 2, id_bufs[0], sems.at[0]).start()

        _id_copy(idx_hbm, c0 + 1, id_bufs[1], sems.at[1]).wait()
        region(id_bufs[1], (2 * k + 1) * ID_CHUNK)
        return carry

    lax.fori_loop(0, chunks_per_step // 2, two_chunks, 0)


def _id_scratch():
    return [pltpu.SMEM((ID_CHUNK, PEER_SLOTS), I32), pltpu.SMEM((ID_CHUNK, PEER_SLOTS), I32),
            pltpu.SemaphoreType.DMA((2,))]


U_GROUP_ORDER = (0, 2, 1, 3)


def _u_raw_row(e):
    return ((e // SUBLANES) * 4 + U_GROUP_ORDER[e % 4]) * SUBLANES + ((e % SUBLANES) // 4) * 4


def _peer_u_kernel(idx_hbm, x_ref, gate_ref, rep_ref, tab_ref, arep_ref, pre_ref, id_a, id_b, sems, *raws):
    tb = x_ref.shape[0]
    rows = raws[0].shape[0] // PEER_SLOTS
    assert rows == 4, "the sublane butterfly below is written for 4 packed rows per expert"
    ones = jnp.ones((SUBLANES, LANES), BF16)
    sub = lax.broadcasted_iota(I32, (SUBLANES, LANES), 0)
    low_pair = (sub % 4) < 2
    even_row = (sub % 2) == 0

    def gather(ids, j, raw_ref):
        _gather_rows(tab_ref, ids, j, raw_ref, _u_raw_row)

    def compute(t, raw_ref):
        x_lo = jnp.concatenate([x_ref[t, pl.ds(0, rows, stride=2), :]] * 2, axis=0)
        x_hi = jnp.concatenate([x_ref[t, pl.ds(1, rows, stride=2), :]] * 2, axis=0)
        merged = []
        for m in range(PEER_SLOTS // SUBLANES):
            a = []
            for j in range(4):
                k = 4 * m + j
                w = raw_ref[k * SUBLANES:(k + 1) * SUBLANES, :]
                p = pltpu.bitcast(w << 16, F32) * x_lo + pltpu.bitcast(w & HI_MASK, F32) * x_hi
                a.append(p + pltpu.roll(p, SUBLANES - 2, axis=0))
            ab = jnp.where(low_pair, a[0], pltpu.roll(a[1], 2, axis=0))
            cd = jnp.where(low_pair, a[2], pltpu.roll(a[3], 2, axis=0))
            ab = ab + pltpu.roll(ab, SUBLANES - 1, axis=0)
            cd = cd + pltpu.roll(cd, SUBLANES - 1, axis=0)
            merged.append(jnp.where(even_row, ab, pltpu.roll(cd, 1, axis=0)))
        p_hi, p_lo = _split2(jnp.concatenate(merged, axis=0))
        sums = _dot_nt(ones, p_hi) + _dot_nt(ones, p_lo)
        pre_ref[pl.ds(t, 1), :] = sums[0:1]

    def region(ids, t_base):
        gather(ids, 0, raws[0])
        for j in range(ID_CHUNK):
            if j + 1 < ID_CHUNK:
                gather(ids, j + 1, raws[(j + 1) % len(raws)])
            compute(t_base + j, raws[j % len(raws)])

    _token_pipeline(idx_hbm, (id_a, id_b), sems, tb, region)
    act = _gelu(pre_ref[...]) * gate_ref[...]
    a1 = act.astype(BF16)
    r1 = act - a1.astype(F32)
    a2, a3 = _split2(r1)
    rep = rep_ref[...]
    arep_ref[...] = _dot(a1, rep) + _dot(a2, rep) + _dot(a3, rep)


def _peer_u(idx, x3, gate, tab, tb):
    t, nch, _ = x3.shape
    rows = tab.shape[0] // (PEER_NKEYS * PEER_NKEYS)
    rep = np.zeros((PEER_SLOTS, PEER_SLOTS * nch), np.float32)
    for e in range(PEER_SLOTS):
        rep[e, e * nch:(e + 1) * nch] = 1.0
    return pl.pallas_call(
        _peer_u_kernel,
        grid=(t // tb,),
        in_specs=[pl.BlockSpec(memory_space=pl.ANY),
                  pl.BlockSpec((tb, nch, LANES), lambda i: (i, 0, 0)),
                  pl.BlockSpec((tb, PEER_SLOTS), lambda i: (i, 0)),
                  pl.BlockSpec(rep.shape, lambda i: (0, 0)),
                  pl.BlockSpec(tab.shape, lambda i: (0, 0), pipeline_mode=pl.Buffered(1))],
        out_specs=pl.BlockSpec((tb, PEER_SLOTS * nch), lambda i: (i, 0)),
        out_shape=jax.ShapeDtypeStruct((t, PEER_SLOTS * nch), F32),
        scratch_shapes=([pltpu.VMEM((tb, PEER_SLOTS), F32)] + _id_scratch()
                        + [pltpu.VMEM((PEER_SLOTS * rows, LANES), I32)] * N_GATHER_BUFS),
        compiler_params=_params(("arbitrary",), 48),
        name="peer_u",
    )(idx, x3, gate, jnp.asarray(rep, BF16), tab)


def _peer_v_kernel(idx_hbm, arep_ref, h_ref, g_ref, tab_ref, o1_ref, o2_ref, y_ref, id_a, id_b, sems, *, n_first):
    tb, nch, _ = h_ref.shape
    rows = tab_ref.shape[0] // (PEER_NKEYS * PEER_NKEYS)
    shape = (nch, arep_ref.shape[1])
    own_chunk = (lax.broadcasted_iota(I32, shape, 1) % nch) == lax.broadcasted_iota(I32, shape, 0)

    def compute(t, ids, j):
        pieces = []
        for e in range(0, PEER_SLOTS, SUBLANES // rows):
            lo = tab_ref[pl.ds(pl.multiple_of(ids[j, e], rows), rows), :]
            hi = tab_ref[pl.ds(pl.multiple_of(ids[j, e + 1], rows), rows), :]
            pieces.append(jnp.concatenate([lo, hi], axis=0))
        vals = pltpu.bitcast(jnp.concatenate(pieces, axis=0), BF16)
        w_hi, w_lo = _split2(jnp.where(own_chunk, arep_ref[pl.ds(t, 1), :], 0.0))
        y_ref[t] = _dot(w_hi, vals) + _dot(w_lo, vals)

    def region(ids, t_base):
        for j in range(ID_CHUNK):
            compute(t_base + j, ids, j)

    _token_pipeline(idx_hbm, (id_a, id_b), sems, tb, region)
    y = h_ref[...] + y_ref[...]
    ms = jnp.sum(jnp.sum(y * y, axis=2, keepdims=True), axis=1, keepdims=True) * (1.0 / (y.shape[1] * y.shape[2]))
    y_ref[...] = y * lax.rsqrt(ms + EPS) * g_ref[...]

    def emit(o_ref):
        for c in range(nch):
            o_ref[:, c * LANES:(c + 1) * LANES] = y_ref[:, c, :]

    @pl.when(pl.program_id(0) < n_first)
    def _():
        emit(o1_ref)

    @pl.when(pl.program_id(0) >= n_first)
    def _():
        emit(o2_ref)


def _peer_v(idx, arep, h3, g3, tab, tb, t_first):
    t, nch, _ = h3.shape
    d = nch * LANES
    n_first = t_first // tb
    blk = pl.BlockSpec((tb, nch, LANES), lambda i: (i, 0, 0))
    return pl.pallas_call(
        functools.partial(_peer_v_kernel, n_first=n_first),
        grid=(t // tb,),
        in_specs=[pl.BlockSpec(memory_space=pl.ANY),
                  pl.BlockSpec((tb, arep.shape[1]), lambda i: (i, 0)),
                  blk,
                  pl.BlockSpec(g3.shape, lambda i: (0, 0, 0)),
                  pl.BlockSpec(tab.shape, lambda i: (0, 0), pipeline_mode=pl.Buffered(1))],
        out_specs=[pl.BlockSpec((tb, d), lambda i: (jnp.minimum(i, n_first - 1), 0)),
                   pl.BlockSpec((tb, d), lambda i: (jnp.maximum(i - n_first, 0), 0))],
        out_shape=[jax.ShapeDtypeStruct((t_first, d), F32), jax.ShapeDtypeStruct((t - t_first, d), F32)],
        scratch_shapes=[pltpu.VMEM((tb, nch, LANES), F32)] + _id_scratch(),
        compiler_params=_params(("arbitrary",), 48),
        name="peer_v",
    )(idx, arep, h3, g3, tab)


def _pick_tile(n, want):
    while n % want:
        want //= 2
    return want


def kernel(x_prompt, x_sample, norm1_g, w_in, pool_w, pool_scale, conv_w, conv_b, filt_w1, filt_b1, filt_f1, filt_w2, filt_b2, filt_f2, filt_w3, hyena_bias, w_pool_out, w_hyena_out, w_o, norm2_g, peer_wq, peer_keys, peer_u, peer_v, normf_g):
    assert norm1_g.shape[0] == 1, "single-layer block"
    assert x_prompt.shape[1:] == x_sample.shape[1:]
    nb_p, seq_len, d = x_prompt.shape
    d_pool = pool_scale.shape[-1]
    d_hy = conv_b.shape[-1]
    d_h = d_hy // 3
    x1 = x_prompt.reshape(-1, d)
    x2 = x_sample.reshape(-1, d)
    tm = _pick_tile(seq_len, 512)

    ahy, sg = _inproj(x1, x2, norm1_g[0], w_in[0].astype(BF16), d_pool + d_hy, tm)
    p1, x0, z, zb = _local(ahy, sg, pool_w[0].astype(BF16), pool_scale[0], conv_w[0], conv_b[0],
                           w_pool_out[0].astype(BF16), seq_len, d_pool, d_hy, tm)
    ks, kd = _filters(filt_w1[0], filt_b1[0], filt_f1[0], filt_w2[0], filt_b2[0], filt_f2[0], filt_w3[0],
                      seq_len, d_h)
    cmat, smat, stmat = _dftgen(seq_len)
    kr, ki = _spectrum(cmat, smat, ks, kd, tm)
    hyb = _long_conv(cmat, smat, stmat, zb, z, x0, kr, ki, hyena_bias[0], seq_len, tm)

    keys = peer_keys[0]
    keys_bf = keys.reshape((-1,) + keys.shape[2:]).astype(BF16)
    h, hn, idx, gate = _route(hyb, p1, sg, x1, x2, w_hyena_out[0].astype(BF16), w_o[0].astype(BF16), norm2_g[0],
                              peer_wq[0].astype(BF16), keys_bf, _pick_tile(seq_len, 256),
                              idx_scale=d // (2 * LANES))

    tb = _pick_tile(seq_len, 256)
    arep = _peer_u(idx, hn, gate, _pack_table(peer_u[0]), tb)
    y_p, y_s = _peer_v(idx, arep, h, normf_g.reshape(1, d // LANES, LANES), _pack_table(peer_v[0]), tb,
                       nb_p * seq_len)
    return (y_p.reshape(x_prompt.shape), y_s.reshape(x_sample.shape))
```

```python
import functools
import math

import jax
import jax.numpy as jnp
import numpy as np
from jax import lax
from jax.experimental import pallas as pl
from jax.experimental.pallas import tpu as pltpu

F32 = jnp.float32
BF16 = jnp.bfloat16
I32 = jnp.int32

EPS = 1e-6
POOL_WINDOWS = (2, 4, 8, 16)
POOL_HALO = 8
PEER_HEADS = 8
PEER_NKEYS = 128
PEER_TOPK = 16
PEER_SLOTS = PEER_HEADS * PEER_TOPK
FILTER_EMB = 33
DECAY_TARGET = 1e-2
FAST_DECAY_PCT = 0.3
SLOW_DECAY_PCT = 1.5
LANES = 128
SUBLANES = 8
HI_MASK = -65536
MIB = 1024 * 1024


def _params(sem, vmem_mib):
    return pltpu.CompilerParams(dimension_semantics=sem, vmem_limit_bytes=vmem_mib * MIB)


def _dot(a, b):
    return jnp.dot(a, b, preferred_element_type=F32)


def _dot_nt(a, b):
    return lax.dot_general(a, b, (((1,), (1,)), ((), ())), preferred_element_type=F32)


def _dot_f32(a, b):
    return jnp.dot(a, b, preferred_element_type=F32, precision=lax.Precision.HIGHEST)


def _rms(x, g):
    return x * lax.rsqrt(jnp.mean(x * x, axis=-1, keepdims=True) + EPS) * g


def _pack_kernel(t_ref, o_ref):
    x = t_ref[...]
    for r in range(x.shape[1] // (2 * LANES)):
        lo = pltpu.bitcast(x[:, (2 * r) * LANES:(2 * r + 1) * LANES].astype(BF16).astype(F32), I32)
        hi = pltpu.bitcast(x[:, (2 * r + 1) * LANES:(2 * r + 2) * LANES].astype(BF16).astype(F32), I32)
        o_ref[:, r * LANES:(r + 1) * LANES] = hi | lax.shift_right_logical(lo, 16)


def _pack_table(tab):
    e, d = tab.shape
    rows = 512
    packed = pl.pallas_call(
        _pack_kernel,
        grid=(e // rows,),
        in_specs=[pl.BlockSpec((rows, d), lambda i: (i, 0))],
        out_specs=pl.BlockSpec((rows, d // 2), lambda i: (i, 0)),
        out_shape=jax.ShapeDtypeStruct((e, d // 2), I32),
        compiler_params=_params(("parallel",), 32),
        name="pack_table",
    )(tab)
    return packed.reshape(e * (d // 2) // LANES, LANES)


def _two_part_specs(tm, d, n_first):
    return [pl.BlockSpec((tm, d), lambda i: (jnp.minimum(i, n_first - 1), 0)),
            pl.BlockSpec((tm, d), lambda i: (jnp.maximum(i - n_first, 0), 0))]


def _two_part_block(x1_ref, x2_ref, n_first):
    return jnp.where(pl.program_id(0) < n_first, x1_ref[...], x2_ref[...])


def _inproj_kernel(x1_ref, x2_ref, g_ref, w_ref, ahy_ref, sg_ref, *, n_ahy, chunk, n_first):
    xn = _rms(_two_part_block(x1_ref, x2_ref, n_first), g_ref[...]).astype(BF16)
    for j in range(n_ahy // chunk):
        ahy_ref[:, j * chunk:(j + 1) * chunk] = _dot(xn, w_ref[:, j * chunk:(j + 1) * chunk])
    n_g = w_ref.shape[1] - n_ahy
    for j in range(n_g // chunk):
        g = _dot(xn, w_ref[:, n_ahy + j * chunk:n_ahy + (j + 1) * chunk])
        sg_ref[:, j * chunk:(j + 1) * chunk] = 1.0 / (1.0 + jnp.exp(-g))


def _inproj(x1, x2, g, w_bf, n_ahy, tm):
    d = x1.shape[1]
    t = x1.shape[0] + x2.shape[0]
    n = w_bf.shape[1]
    n_first = x1.shape[0] // tm
    return pl.pallas_call(
        functools.partial(_inproj_kernel, n_ahy=n_ahy, chunk=min(1024, n_ahy, n - n_ahy), n_first=n_first),
        grid=(t // tm,),
        in_specs=_two_part_specs(tm, d, n_first) + [
                  pl.BlockSpec((1, d), lambda i: (0, 0)),
                  pl.BlockSpec((d, n), lambda i: (0, 0))],
        out_specs=[pl.BlockSpec((tm, n_ahy), lambda i: (i, 0)),
                   pl.BlockSpec((tm, n - n_ahy), lambda i: (i, 0))],
        out_shape=[jax.ShapeDtypeStruct((t, n_ahy), F32), jax.ShapeDtypeStruct((t, n - n_ahy), F32)],
        compiler_params=_params(("parallel",), 56),
        name="inproj",
    )(x1, x2, g.reshape(1, d), w_bf)


def _local_kernel(ahy_ref, prev_ref, next_ref, sga_ref, pw_ref, ps_ref, cw_ref, cb_ref, wpo_ref,
                  p1_ref, x0_ref, z_ref, zb_ref, *, seq_len, d_pool, d_hy):
    tm = ahy_ref.shape[0]
    ext_rows = tm + 2 * POOL_HALO
    tiles_per_seq = seq_len // tm
    j = pl.program_id(0) % tiles_per_seq
    prev = jnp.where(j == 0, 0.0, prev_ref[...])
    nxt = jnp.where(j == tiles_per_seq - 1, 0.0, next_ref[...])
    pos = j * tm + lax.broadcasted_iota(I32, (tm, 1), 0)

    def ext(lo, hi):
        return jnp.concatenate([prev[:, lo:hi], ahy_ref[:, lo:hi], nxt[:, lo:hi]], axis=0)

    def shifted(v, k):
        return pltpu.roll(v, k % ext_rows, axis=0)

    group = d_pool // len(POOL_WINDOWS)
    mixed = []
    for g, w in enumerate(POOL_WINDOWS):
        half = w // 2
        e = ext(g * group, (g + 1) * group)
        win = e + shifted(e, 1)
        span = 1
        while span < half:
            win = shifted(win, span) + shifted(win, -span)
            span *= 2
        win = win[POOL_HALO:POOL_HALO + tm]
        cnt = (jnp.minimum(pos + half, seq_len) - jnp.maximum(pos - half, 0)).astype(F32)
        pooled = win / cnt - ahy_ref[:, g * group:(g + 1) * group]
        mixed.append(_dot(pooled.astype(BF16), pw_ref[g]))
    pm = jnp.concatenate(mixed, axis=1) * ps_ref[...]
    p1_ref[...] = sga_ref[...] * _dot(pm.astype(BF16), wpo_ref[...])

    d_br = d_hy // 3
    branches = []
    for b in range(3):
        lo, hi = d_pool + b * d_br, d_pool + (b + 1) * d_br
        e = ext(lo, hi)
        c0, c1 = b * d_br, (b + 1) * d_br
        uc = (shifted(e, 1)[POOL_HALO:POOL_HALO + tm] * cw_ref[0:1, c0:c1]
              + ahy_ref[:, lo:hi] * cw_ref[1:2, c0:c1]
              + shifted(e, -1)[POOL_HALO:POOL_HALO + tm] * cw_ref[2:3, c0:c1]
              + cb_ref[:, c0:c1])
        branches.append(uc)
    x0, x1, v = branches
    z = v * x1
    x0_ref[...] = x0
    z_ref[...] = z
    zb_ref[...] = z.astype(BF16)


def _local(ahy, sg, pool_w_bf, pool_scale, conv_w, conv_b, w_pool_out_bf, seq_len, d_pool, d_hy, tm):
    t, n_ahy = ahy.shape
    d_model = w_pool_out_bf.shape[1]
    d_br = d_hy // 3
    hb = tm // POOL_HALO
    n_halo_blocks = t // POOL_HALO
    return pl.pallas_call(
        functools.partial(_local_kernel, seq_len=seq_len, d_pool=d_pool, d_hy=d_hy),
        grid=(t // tm,),
        in_specs=[pl.BlockSpec((tm, n_ahy), lambda i: (i, 0)),
                  pl.BlockSpec((POOL_HALO, n_ahy), lambda i: (jnp.maximum(i * hb - 1, 0), 0)),
                  pl.BlockSpec((POOL_HALO, n_ahy), lambda i: (jnp.minimum((i + 1) * hb, n_halo_blocks - 1), 0)),
                  pl.BlockSpec((tm, d_model), lambda i: (i, 0)),
                  pl.BlockSpec(pool_w_bf.shape, lambda i: (0, 0, 0)),
                  pl.BlockSpec((1, d_pool), lambda i: (0, 0)),
                  pl.BlockSpec(conv_w.shape, lambda i: (0, 0)),
                  pl.BlockSpec((1, d_hy), lambda i: (0, 0)),
                  pl.BlockSpec(w_pool_out_bf.shape, lambda i: (0, 0))],
        out_specs=[pl.BlockSpec((tm, d_model), lambda i: (i, 0)),
                   pl.BlockSpec((tm, d_br), lambda i: (i, 0)),
                   pl.BlockSpec((tm, d_br), lambda i: (i, 0)),
                   pl.BlockSpec((tm, d_br), lambda i: (i, 0))],
        out_shape=[jax.ShapeDtypeStruct((t, d_model), F32),
                   jax.ShapeDtypeStruct((t, d_br), F32),
                   jax.ShapeDtypeStruct((t, d_br), F32),
                   jax.ShapeDtypeStruct((t, d_br), BF16)],
        compiler_params=_params(("parallel",), 56),
        name="local_mix",
    )(ahy, ahy, ahy, sg, pool_w_bf, pool_scale.reshape(1, d_pool), conv_w, conv_b.reshape(1, d_hy), w_pool_out_bf)


def _filter_kernel(freq_ref, w1_ref, b1_ref, f1_ref, w2_ref, b2_ref, f2_ref, w3_ref, dl_ref, ks_ref, kd_ref,
                   *, seq_len):
    tl = ks_ref.shape[0]
    d_h = ks_ref.shape[1]
    i = (pl.program_id(0) * tl + lax.broadcasted_iota(I32, (tl, 1), 0)).astype(F32)
    t = i / (seq_len - 1.0)
    wpos = (2.0 * math.pi) * i / seq_len
    lane = lax.broadcasted_iota(I32, (tl, LANES), 1)
    bands = (FILTER_EMB - 1) // 2
    ang = freq_ref[...] * wpos
    feat = jnp.where(lane == 0, t,
                     jnp.where(lane <= bands, jnp.cos(ang),
                               jnp.where(lane <= 2 * bands, -jnp.sin(ang), 0.0)))
    h = jnp.sin(f1_ref[...] * (_dot_f32(feat, w1_ref[...]) + b1_ref[...]))
    h = jnp.sin(f2_ref[...] * (_dot_f32(h, w2_ref[...]) + b2_ref[...]))
    h = _dot_f32(h, w3_ref[...])
    decay = jnp.exp(-t * jnp.abs(dl_ref[...]))
    kf = h[:, :d_h] * decay
    kb = h[:, d_h:] * decay
    ks_ref[...] = kf + kb
    kd_ref[...] = kf - kb


def _filters(w1, b1, f1, w2, b2, f2, w3, seq_len, d_h):
    hid = w1.shape[1]
    bands = (FILTER_EMB - 1) // 2
    freqs = np.zeros((1, LANES), np.float32)
    fr = np.linspace(1e-4, bands - 1, bands, dtype=np.float32)
    freqs[0, 1:1 + bands] = fr
    freqs[0, 1 + bands:1 + 2 * bands] = fr
    w1p = jnp.zeros((LANES, hid), F32).at[:FILTER_EMB].set(w1)
    max_decay = math.log(DECAY_TARGET) / FAST_DECAY_PCT
    min_decay = math.log(DECAY_TARGET) / SLOW_DECAY_PCT
    deltas = np.linspace(min_decay, max_decay, d_h, dtype=np.float32).reshape(1, d_h)
    tl = min(512, seq_len)
    full = lambda shape: pl.BlockSpec(shape, lambda i: tuple(0 for _ in shape))
    return pl.pallas_call(
        functools.partial(_filter_kernel, seq_len=seq_len),
        grid=(seq_len // tl,),
        in_specs=[full((1, LANES)), full((LANES, hid)), full((1, hid)), full((1, hid)),
                  full((hid, hid)), full((1, hid)), full((1, hid)), full((hid, 2 * d_h)), full((1, d_h))],
        out_specs=[pl.BlockSpec((tl, d_h), lambda i: (i, 0)), pl.BlockSpec((tl, d_h), lambda i: (i, 0))],
        out_shape=[jax.ShapeDtypeStruct((seq_len, d_h), F32), jax.ShapeDtypeStruct((seq_len, d_h), F32)],
        compiler_params=_params(("parallel",), 32),
        name="hyena_filters",
    )(jnp.asarray(freqs), w1p, b1.reshape(1, hid), f1.reshape(1, hid), w2, b2.reshape(1, hid),
      f2.reshape(1, hid), w3, jnp.asarray(deltas))


def _dftgen_kernel(c_ref, s_ref, st_ref, *, n_fft):
    tr, n = c_ref.shape
    row = pl.program_id(0) * tr + lax.broadcasted_iota(I32, (tr, n), 0)
    col = lax.broadcasted_iota(I32, (tr, n), 1)
    ang = ((row * col) & (n_fft - 1)).astype(F32) * (2.0 * math.pi / n_fft)
    c_ref[...] = jnp.cos(ang).astype(BF16)
    ms = -jnp.sin(ang)
    sign_col = (1 - 2 * (col & 1)).astype(F32)
    sign_row = (1 - 2 * (row & 1)).astype(F32)
    s_ref[...] = jnp.where(row == 0, sign_col, ms).astype(BF16)
    st_ref[...] = jnp.where(col == 0, sign_row, ms).astype(BF16)


def _dftgen(seq_len):
    tr = min(256, seq_len)
    spec = pl.BlockSpec((tr, seq_len), lambda i: (i, 0))
    shp = jax.ShapeDtypeStruct((seq_len, seq_len), BF16)
    return pl.pallas_call(
        functools.partial(_dftgen_kernel, n_fft=2 * seq_len),
        grid=(seq_len // tr,),
        out_specs=[spec, spec, spec],
        out_shape=[shp, shp, shp],
        compiler_params=_params(("parallel",), 48),
        name="dft_matrices",
    )()


def _spectrum_kernel(c_ref, s_ref, ks_ref, kd_ref, kr_ref, ki_ref):
    ks = ks_ref[...]
    kr_ref[...] = _dot(c_ref[...], ks.astype(BF16))
    ki = _dot(s_ref[...], kd_ref[...].astype(BF16))
    tf = ki.shape[0]
    row = lax.broadcasted_iota(I32, (tf, 1), 0)
    sign = (1 - 2 * (lax.broadcasted_iota(I32, (ks.shape[0], 1), 0) & 1)).astype(F32)
    nyq = jnp.sum(ks * sign, axis=0, keepdims=True)
    first = jnp.logical_and(pl.program_id(0) == 0, row == 0)
    ki_ref[...] = jnp.where(first, nyq, ki)


def _spectrum(cmat, smat, ks, kd, tf):
    n, d_h = ks.shape
    return pl.pallas_call(
        _spectrum_kernel,
        grid=(n // tf,),
        in_specs=[pl.BlockSpec((tf, n), lambda i: (i, 0)), pl.BlockSpec((tf, n), lambda i: (i, 0)),
                  pl.BlockSpec((n, d_h), lambda i: (0, 0)), pl.BlockSpec((n, d_h), lambda i: (0, 0))],
        out_specs=[pl.BlockSpec((tf, d_h), lambda i: (i, 0)), pl.BlockSpec((tf, d_h), lambda i: (i, 0))],
        out_shape=[jax.ShapeDtypeStruct((n, d_h), F32), jax.ShapeDtypeStruct((n, d_h), F32)],
        compiler_params=_params(("parallel",), 56),
        name="filter_spectrum",
    )(cmat, smat, ks, kd)


def _fwd_kernel(c_ref, s_ref, z_ref, kr_ref, ki_ref, yr_ref, yi_ref, *, n_fft):
    z = z_ref[...]
    zr = _dot(c_ref[...], z)
    zi = _dot(s_ref[...], z)
    kr = kr_ref[...]
    ki = ki_ref[...]
    row = lax.broadcasted_iota(I32, (zr.shape[0], 1), 0)
    first = jnp.logical_and(pl.program_id(0) == 0, row == 0)
    yr = jnp.where(first, zr * kr, zr * kr - zi * ki)
    yi = jnp.where(first, zi * ki, zr * ki + zi * kr)
    scale = jnp.where(first, 1.0 / n_fft, 2.0 / n_fft)
    yr_ref[...] = (yr * scale).astype(BF16)
    yi_ref[...] = (yi * scale).astype(BF16)


def _inv_kernel(c_ref, st_ref, yr_ref, yi_ref, z_ref, x0_ref, bias_ref, o_ref):
    y = _dot(c_ref[...], yr_ref[...]) + _dot(st_ref[...], yi_ref[...])
    o_ref[...] = ((y + z_ref[...] * bias_ref[...]) * x0_ref[...]).astype(BF16)


def _long_conv(cmat, smat, stmat, zb, z, x0, kr, ki, bias, seq_len, tf):
    t, d_h = z.shape
    nb = t // seq_len
    nf = seq_len // tf
    mat_spec = pl.BlockSpec((tf, seq_len), lambda f, b: (f, 0))
    seq_spec = pl.BlockSpec((seq_len, d_h), lambda f, b: (b, 0))
    tile_spec = pl.BlockSpec((tf, d_h), lambda f, b: (b * nf + f, 0))
    k_spec = pl.BlockSpec((tf, d_h), lambda f, b: (f, 0))
    yr, yi = pl.pallas_call(
        functools.partial(_fwd_kernel, n_fft=2 * seq_len),
        grid=(nf, nb),
        in_specs=[mat_spec, mat_spec, seq_spec, k_spec, k_spec],
        out_specs=[tile_spec, tile_spec],
        out_shape=[jax.ShapeDtypeStruct((t, d_h), BF16), jax.ShapeDtypeStruct((t, d_h), BF16)],
        compiler_params=_params(("parallel", "parallel"), 56),
        name="conv_fwd_dft",
    )(cmat, smat, zb, kr, ki)
    return pl.pallas_call(
        _inv_kernel,
        grid=(nf, nb),
        in_specs=[mat_spec, mat_spec, seq_spec, seq_spec, tile_spec, tile_spec,
                  pl.BlockSpec((1, d_h), lambda f, b: (0, 0))],
        out_specs=tile_spec,
        out_shape=jax.ShapeDtypeStruct((t, d_h), BF16),
        compiler_params=_params(("parallel", "parallel"), 56),
        name="conv_inv_dft",
    )(cmat, stmat, yr, yi, z, x0, bias.reshape(1, d_h))


def _top16(s, payload=None):
    n_rows, n = s.shape
    groups = n_rows // SUBLANES
    slabs = [s[g * SUBLANES:(g + 1) * SUBLANES] for g in range(groups)]
    sub = lax.broadcasted_iota(I32, (SUBLANES, n), 0).astype(F32)
    rid = [sub + float(g * SUBLANES) for g in range(groups)]
    vals, ids = [], []
    for _ in range(PEER_TOPK):
        m = jnp.max(functools.reduce(jnp.maximum, slabs), axis=0, keepdims=True)
        first = jnp.full((SUBLANES, n), float(groups), F32)
        for g in reversed(range(groups)):
            first = jnp.where(slabs[g] == m, float(g), first)
        row = jnp.where(first < float(groups), first * float(SUBLANES) + sub, float(n_rows))
        pos = jnp.min(row, axis=0, keepdims=True)
        hits = [r == pos for r in rid]
        vals.append(m)
        if payload is None:
            ids.append(pos)
        else:
            picked = [jnp.where(hits[g], payload[g * SUBLANES:(g + 1) * SUBLANES], -1.0) for g in range(groups)]
            ids.append(jnp.max(functools.reduce(jnp.maximum, picked), axis=0, keepdims=True))
        slabs = [jnp.where(h, -jnp.inf, sl) for h, sl in zip(hits, slabs)]
    return jnp.concatenate(vals, axis=0), jnp.concatenate(ids, axis=0)


def _route_kernel(hyb_ref, p1_ref, sgb_ref, x1_ref, x2_ref, who_ref, wo_ref, g2_ref, wq_ref, keys_ref,
                  h_ref, hn_ref, idx_ref, gate_ref, q_scr, idx_scr, gate_scr, cand_scr, cidx_scr,
                  *, idx_scale, n_first):
    yb = _dot(hyb_ref[...], who_ref[...])
    mixed = p1_ref[...] + sgb_ref[...] * yb
    h = _two_part_block(x1_ref, x2_ref, n_first) + _dot(mixed.astype(BF16), wo_ref[...])
    hn = _rms(h, g2_ref[...])
    for c in range(h.shape[1] // LANES):
        h_ref[:, c, :] = h[:, c * LANES:(c + 1) * LANES]
        hn_ref[:, c, :] = hn[:, c * LANES:(c + 1) * LANES]
    q = _dot(hn.astype(BF16), wq_ref[...])
    n_half = 2 * PEER_HEADS
    dh = q.shape[1] // n_half
    for j in range(n_half):
        q_scr[j] = q[:, j * dh:(j + 1) * dh].astype(BF16)

    def head(hd, carry):
        sv0, si0 = _top16(_dot_nt(keys_ref[2 * hd], q_scr[2 * hd]))
        sv1, si1 = _top16(_dot_nt(keys_ref[2 * hd + 1], q_scr[2 * hd + 1]))
        cand_scr[...] = jnp.full(cand_scr.shape, -jnp.inf, F32)
        cidx_scr[...] = jnp.full(cidx_scr.shape, -1.0, F32)
        off = 0
        for a in range(PEER_TOPK):
            nb = PEER_TOPK // (a + 1)
            cand_scr[off:off + nb, :] = sv0[a:a + 1] + sv1[:nb]
            cidx_scr[off:off + nb, :] = (si0[a:a + 1] * float(PEER_NKEYS) + si1[:nb]) * float(idx_scale)
            off += nb
        best, eid = _top16(cand_scr[...], cidx_scr[...])
        ex = jnp.exp(best - best[0:1])
        r0 = pl.multiple_of(hd * PEER_TOPK, PEER_TOPK)
        idx_scr[pl.ds(r0, PEER_TOPK), :] = eid.astype(I32)
        gate_scr[pl.ds(r0, PEER_TOPK), :] = ex / jnp.sum(ex, axis=0, keepdims=True)
        return carry

    lax.fori_loop(0, PEER_HEADS, head, 0)
    idx_ref[...] = idx_scr[...].T
    gate_ref[...] = gate_scr[...].T


def _route(hyb, p1, sg, x1, x2, who_bf, wo_bf, g2, wq_bf, keys_bf, tm, idx_scale):
    d = x1.shape[1]
    t = x1.shape[0] + x2.shape[0]
    n_first = x1.shape[0] // tm
    d_h = hyb.shape[1]
    n_half, nk, dh = keys_bf.shape
    n_cand = sum(PEER_TOPK // (a + 1) for a in range(PEER_TOPK))
    n_cand = -(-n_cand // SUBLANES) * SUBLANES
    tok = lambda w: pl.BlockSpec((tm, w), lambda i: (i, 0))
    full = lambda a: pl.BlockSpec(a.shape, lambda i: tuple(0 for _ in a.shape))
    tok3 = pl.BlockSpec((tm, d // LANES, LANES), lambda i: (i, 0, 0))
    return pl.pallas_call(
        functools.partial(_route_kernel, idx_scale=idx_scale, n_first=n_first),
        grid=(t // tm,),
        in_specs=[tok(d_h), tok(d), pl.BlockSpec((tm, d), lambda i: (i, 1))] + _two_part_specs(tm, d, n_first) + [
                  full(who_bf), full(wo_bf), pl.BlockSpec((1, d), lambda i: (0, 0)), full(wq_bf), full(keys_bf)],
        out_specs=[tok3, tok3, tok(PEER_SLOTS), tok(PEER_SLOTS)],
        out_shape=[jax.ShapeDtypeStruct((t, d // LANES, LANES), F32), jax.ShapeDtypeStruct((t, d // LANES, LANES), F32),
                   jax.ShapeDtypeStruct((t, PEER_SLOTS), I32), jax.ShapeDtypeStruct((t, PEER_SLOTS), F32)],
        scratch_shapes=[pltpu.VMEM((n_half, tm, dh), BF16),
                        pltpu.VMEM((PEER_SLOTS, tm), I32),
                        pltpu.VMEM((PEER_SLOTS, tm), F32),
                        pltpu.VMEM((n_cand, tm), F32),
                        pltpu.VMEM((n_cand, tm), F32)],
        compiler_params=_params(("parallel",), 56),
        name="route",
    )(hyb, p1, sg, x1, x2, who_bf, wo_bf, g2.reshape(1, d), wq_bf, keys_bf)


def _slot_pair(tab_ref, ids, j, e_a, e_b):
    rows = tab_ref.shape[0] // (PEER_NKEYS * PEER_NKEYS)
    assert 2 * rows == SUBLANES
    lo = tab_ref[pl.ds(pl.multiple_of(ids[j, e_a], rows), rows), :]
    hi = tab_ref[pl.ds(pl.multiple_of(ids[j, e_b], rows), rows), :]
    return jnp.concatenate([lo, hi], axis=0)


def _split2(x):
    hi = x.astype(BF16)
    return hi, (x - hi.astype(F32)).astype(BF16)


def _gelu(x):
    return 0.5 * x * (1.0 + jnp.tanh(math.sqrt(2.0 / math.pi) * (x + 0.044715 * (x * x * x))))


ID_CHUNK = 32


def _id_copy(idx_hbm, chunk, dst, sem):
    return pltpu.make_async_copy(idx_hbm.at[pl.ds(chunk * ID_CHUNK, ID_CHUNK), :], dst, sem)


def _token_pipeline(idx_hbm, id_bufs, sems, tb, region):
    chunks_per_step = tb // ID_CHUNK
    assert chunks_per_step % 2 == 0
    first = pl.program_id(0) * chunks_per_step
    total = pl.num_programs(0) * chunks_per_step

    def two_chunks(k, carry):
        c0 = first + 2 * k

        @pl.when(c0 == 0)
        def _():
            _id_copy(idx_hbm, c0, id_bufs[0], sems.at[0]).start()

        _id_copy(idx_hbm, c0 + 1, id_bufs[1], sems.at[1]).start()
        _id_copy(idx_hbm, c0, id_bufs[0], sems.at[0]).wait()
        region(id_bufs[0], 2 * k * ID_CHUNK)

        @pl.when(c0 + 2 < total)
        def _():
            _id_copy(idx_hbm, c0 + 2, id_bufs[0], sems.at[0]).start()

        _id_copy(idx_hbm, c0 + 1, id_bufs[1], sems.at[1]).wait()
        region(id_bufs[1], (2 * k + 1) * ID_CHUNK)
        return carry

    lax.fori_loop(0, chunks_per_step // 2, two_chunks, 0)


def _id_scratch():
    return [pltpu.SMEM((ID_CHUNK, PEER_SLOTS), I32), pltpu.SMEM((ID_CHUNK, PEER_SLOTS), I32),
            pltpu.SemaphoreType.DMA((2,))]


U_GROUP_ORDER = (0, 2, 1, 3)


def _peer_u_kernel(idx_hbm, x_ref, gate_ref, rep_ref, tab_ref, arep_ref, pre_ref, id_a, id_b, sems):
    tb = x_ref.shape[0]
    rows = tab_ref.shape[0] // (PEER_NKEYS * PEER_NKEYS)
    assert rows == 4, "the sublane butterfly below is written for 4 packed rows per expert"
    ones = jnp.ones((SUBLANES, LANES), BF16)
    sub = lax.broadcasted_iota(I32, (SUBLANES, LANES), 0)
    low_pair = (sub % 4) < 2
    even_row = (sub % 2) == 0

    def compute(t, ids, jt):
        x_lo = jnp.concatenate([x_ref[t, pl.ds(0, rows, stride=2), :]] * 2, axis=0)
        x_hi = jnp.concatenate([x_ref[t, pl.ds(1, rows, stride=2), :]] * 2, axis=0)
        merged = []
        for m in range(PEER_SLOTS // SUBLANES):
            a = []
            for j in range(4):
                e = SUBLANES * m + U_GROUP_ORDER[j]
                w = _slot_pair(tab_ref, ids, jt, e, e + 4)
                p = pltpu.bitcast(w << 16, F32) * x_lo + pltpu.bitcast(w & HI_MASK, F32) * x_hi
                a.append(p + pltpu.roll(p, SUBLANES - 2, axis=0))
            ab = jnp.where(low_pair, a[0], pltpu.roll(a[1], 2, axis=0))
            cd = jnp.where(low_pair, a[2], pltpu.roll(a[3], 2, axis=0))
            ab = ab + pltpu.roll(ab, SUBLANES - 1, axis=0)
            cd = cd + pltpu.roll(cd, SUBLANES - 1, axis=0)
            merged.append(jnp.where(even_row, ab, pltpu.roll(cd, 1, axis=0)))
        p_hi, p_lo = _split2(jnp.concatenate(merged, axis=0))
        sums = _dot_nt(ones, p_hi) + _dot_nt(ones, p_lo)
        pre_ref[pl.ds(t, 1), :] = sums[0:1]

    def region(ids, t_base):
        for j in range(ID_CHUNK):
            compute(t_base + j, ids, j)

    _token_pipeline(idx_hbm, (id_a, id_b), sems, tb, region)
    act = _gelu(pre_ref[...]) * gate_ref[...]
    a1 = act.astype(BF16)
    r1 = act - a1.astype(F32)
    a2, a3 = _split2(r1)
    rep = rep_ref[...]
    arep_ref[...] = _dot(a1, rep) + _dot(a2, rep) + _dot(a3, rep)


def _peer_u(idx, x3, gate, tab, tb):
    t, nch, _ = x3.shape
    rep = np.zeros((PEER_SLOTS, PEER_SLOTS * nch), np.float32)
    for e in range(PEER_SLOTS):
        rep[e, e * nch:(e + 1) * nch] = 1.0
    return pl.pallas_call(
        _peer_u_kernel,
        grid=(t // tb,),
        in_specs=[pl.BlockSpec(memory_space=pl.ANY),
                  pl.BlockSpec((tb, nch, LANES), lambda i: (i, 0, 0)),
                  pl.BlockSpec((tb, PEER_SLOTS), lambda i: (i, 0)),
                  pl.BlockSpec(rep.shape, lambda i: (0, 0)),
                  pl.BlockSpec(tab.shape, lambda i: (0, 0), pipeline_mode=pl.Buffered(1))],
        out_specs=pl.BlockSpec((tb, PEER_SLOTS * nch), lambda i: (i, 0)),
        out_shape=jax.ShapeDtypeStruct((t, PEER_SLOTS * nch), F32),
        scratch_shapes=[pltpu.VMEM((tb, PEER_SLOTS), F32)] + _id_scratch(),
        compiler_params=_params(("arbitrary",), 48),
        name="peer_u",
    )(idx, x3, gate, jnp.asarray(rep, BF16), tab)


def _peer_v_kernel(idx_hbm, arep_ref, h_ref, g_ref, tab_ref, o1_ref, o2_ref, y_ref, id_a, id_b, sems, *, n_first):
    tb, nch, _ = h_ref.shape
    shape = (nch, arep_ref.shape[1])
    own_chunk = (lax.broadcasted_iota(I32, shape, 1) % nch) == lax.broadcasted_iota(I32, shape, 0)

    def compute(t, ids, j):
        pieces = [_slot_pair(tab_ref, ids, j, e, e + 1) for e in range(0, PEER_SLOTS, 2)]
        vals = pltpu.bitcast(jnp.concatenate(pieces, axis=0), BF16)
        w_hi, w_lo = _split2(jnp.where(own_chunk, arep_ref[pl.ds(t, 1), :], 0.0))
        y_ref[t] = _dot(w_hi, vals) + _dot(w_lo, vals)

    def region(ids, t_base):
        for j in range(ID_CHUNK):
            compute(t_base + j, ids, j)

    _token_pipeline(idx_hbm, (id_a, id_b), sems, tb, region)
    y = h_ref[...] + y_ref[...]
    ms = jnp.sum(jnp.sum(y * y, axis=2, keepdims=True), axis=1, keepdims=True) * (1.0 / (y.shape[1] * y.shape[2]))
    y_ref[...] = y * lax.rsqrt(ms + EPS) * g_ref[...]

    def emit(o_ref):
        for c in range(nch):
            o_ref[:, c * LANES:(c + 1) * LANES] = y_ref[:, c, :]

    @pl.when(pl.program_id(0) < n_first)
    def _():
        emit(o1_ref)

    @pl.when(pl.program_id(0) >= n_first)
    def _():
        emit(o2_ref)


def _peer_v(idx, arep, h3, g3, tab, tb, t_first):
    t, nch, _ = h3.shape
    d = nch * LANES
    n_first = t_first // tb
    blk = pl.BlockSpec((tb, nch, LANES), lambda i: (i, 0, 0))
    return pl.pallas_call(
        functools.partial(_peer_v_kernel, n_first=n_first),
        grid=(t // tb,),
        in_specs=[pl.BlockSpec(memory_space=pl.ANY),
                  pl.BlockSpec((tb, arep.shape[1]), lambda i: (i, 0)),
                  blk,
                  pl.BlockSpec(g3.shape, lambda i: (0, 0, 0)),
                  pl.BlockSpec(tab.shape, lambda i: (0, 0), pipeline_mode=pl.Buffered(1))],
        out_specs=[pl.BlockSpec((tb, d), lambda i: (jnp.minimum(i, n_first - 1), 0)),
                   pl.BlockSpec((tb, d), lambda i: (jnp.maximum(i - n_first, 0), 0))],
        out_shape=[jax.ShapeDtypeStruct((t_first, d), F32), jax.ShapeDtypeStruct((t - t_first, d), F32)],
        scratch_shapes=[pltpu.VMEM((tb, nch, LANES), F32)] + _id_scratch(),
        compiler_params=_params(("arbitrary",), 48),
        name="peer_v",
    )(idx, arep, h3, g3, tab)


def _pick_tile(n, want):
    while n % want:
        want //= 2
    return want


def kernel(x_prompt, x_sample, norm1_g, w_in, pool_w, pool_scale, conv_w, conv_b, filt_w1, filt_b1, filt_f1, filt_w2, filt_b2, filt_f2, filt_w3, hyena_bias, w_pool_out, w_hyena_out, w_o, norm2_g, peer_wq, peer_keys, peer_u, peer_v, normf_g):
    assert norm1_g.shape[0] == 1, "single-layer block"
    assert x_prompt.shape[1:] == x_sample.shape[1:]
    nb_p, seq_len, d = x_prompt.shape
    d_pool = pool_scale.shape[-1]
    d_hy = conv_b.shape[-1]
    d_h = d_hy // 3
    x1 = x_prompt.reshape(-1, d)
    x2 = x_sample.reshape(-1, d)
    tm = _pick_tile(seq_len, 512)

    ahy, sg = _inproj(x1, x2, norm1_g[0], w_in[0].astype(BF16), d_pool + d_hy, tm)
    p1, x0, z, zb = _local(ahy, sg, pool_w[0].astype(BF16), pool_scale[0], conv_w[0], conv_b[0],
                           w_pool_out[0].astype(BF16), seq_len, d_pool, d_hy, tm)
    ks, kd = _filters(filt_w1[0], filt_b1[0], filt_f1[0], filt_w2[0], filt_b2[0], filt_f2[0], filt_w3[0],
                      seq_len, d_h)
    cmat, smat, stmat = _dftgen(seq_len)
    kr, ki = _spectrum(cmat, smat, ks, kd, tm)
    hyb = _long_conv(cmat, smat, stmat, zb, z, x0, kr, ki, hyena_bias[0], seq_len, tm)

    keys = peer_keys[0]
    keys_bf = keys.reshape((-1,) + keys.shape[2:]).astype(BF16)
    h, hn, idx, gate = _route(hyb, p1, sg, x1, x2, w_hyena_out[0].astype(BF16), w_o[0].astype(BF16), norm2_g[0],
                              peer_wq[0].astype(BF16), keys_bf, _pick_tile(seq_len, 512),
                              idx_scale=d // (2 * LANES))

    tb = _pick_tile(seq_len, 256)
    arep = _peer_u(idx, hn, gate, _pack_table(peer_u[0]), tb)
    y_p, y_s = _peer_v(idx, arep, h, normf_g.reshape(1, d // LANES, LANES), _pack_table(peer_v[0]), tb,
                       nb_p * seq_len)
    return (y_p.reshape(x_prompt.shape), y_s.reshape(x_sample.shape))
```

```python
import functools
import math

import jax
import jax.numpy as jnp
import numpy as np
from jax import lax
from jax.experimental import pallas as pl
from jax.experimental.pallas import tpu as pltpu

F32 = jnp.float32
BF16 = jnp.bfloat16
I32 = jnp.int32

EPS = 1e-6
POOL_WINDOWS = (2, 4, 8, 16)
POOL_HALO = 8
PEER_HEADS = 8
PEER_NKEYS = 128
PEER_TOPK = 16
PEER_SLOTS = PEER_HEADS * PEER_TOPK
FILTER_EMB = 33
DECAY_TARGET = 1e-2
FAST_DECAY_PCT = 0.3
SLOW_DECAY_PCT = 1.5
LANES = 128
SUBLANES = 8
HI_MASK = -65536
MIB = 1024 * 1024


def _params(sem, vmem_mib):
    return pltpu.CompilerParams(dimension_semantics=sem, vmem_limit_bytes=vmem_mib * MIB)


def _dot(a, b):
    return jnp.dot(a, b, preferred_element_type=F32)


def _dot_nt(a, b):
    return lax.dot_general(a, b, (((1,), (1,)), ((), ())), preferred_element_type=F32)


def _dot_f32(a, b):
    return jnp.dot(a, b, preferred_element_type=F32, precision=lax.Precision.HIGHEST)


def _rms(x, g):
    return x * lax.rsqrt(jnp.mean(x * x, axis=-1, keepdims=True) + EPS) * g


def _pack_kernel(t_ref, o_ref):
    x = t_ref[...]
    for r in range(x.shape[1] // (2 * LANES)):
        lo = pltpu.bitcast(x[:, (2 * r) * LANES:(2 * r + 1) * LANES].astype(BF16).astype(F32), I32)
        hi = pltpu.bitcast(x[:, (2 * r + 1) * LANES:(2 * r + 2) * LANES].astype(BF16).astype(F32), I32)
        o_ref[:, r * LANES:(r + 1) * LANES] = hi | lax.shift_right_logical(lo, 16)


def _pack_table(tab):
    e, d = tab.shape
    rows = 512
    packed = pl.pallas_call(
        _pack_kernel,
        grid=(e // rows,),
        in_specs=[pl.BlockSpec((rows, d), lambda i: (i, 0))],
        out_specs=pl.BlockSpec((rows, d // 2), lambda i: (i, 0)),
        out_shape=jax.ShapeDtypeStruct((e, d // 2), I32),
        compiler_params=_params(("parallel",), 32),
        name="pack_table",
    )(tab)
    return packed.reshape(e * (d // 2) // LANES, LANES)


def _two_part_specs(tm, d, n_first):
    return [pl.BlockSpec((tm, d), lambda i: (jnp.minimum(i, n_first - 1), 0)),
            pl.BlockSpec((tm, d), lambda i: (jnp.maximum(i - n_first, 0), 0))]


def _two_part_block(x1_ref, x2_ref, n_first):
    return jnp.where(pl.program_id(0) < n_first, x1_ref[...], x2_ref[...])


def _inproj_kernel(x1_ref, x2_ref, g_ref, w_ref, ahy_ref, sg_ref, *, n_ahy, chunk, n_first):
    xn = _rms(_two_part_block(x1_ref, x2_ref, n_first), g_ref[...]).astype(BF16)
    for j in range(n_ahy // chunk):
        ahy_ref[:, j * chunk:(j + 1) * chunk] = _dot(xn, w_ref[:, j * chunk:(j + 1) * chunk])
    n_g = w_ref.shape[1] - n_ahy
    for j in range(n_g // chunk):
        g = _dot(xn, w_ref[:, n_ahy + j * chunk:n_ahy + (j + 1) * chunk])
        sg_ref[:, j * chunk:(j + 1) * chunk] = 1.0 / (1.0 + jnp.exp(-g))


def _inproj(x1, x2, g, w_bf, n_ahy, tm):
    d = x1.shape[1]
    t = x1.shape[0] + x2.shape[0]
    n = w_bf.shape[1]
    n_first = x1.shape[0] // tm
    return pl.pallas_call(
        functools.partial(_inproj_kernel, n_ahy=n_ahy, chunk=min(1024, n_ahy, n - n_ahy), n_first=n_first),
        grid=(t // tm,),
        in_specs=_two_part_specs(tm, d, n_first) + [
                  pl.BlockSpec((1, d), lambda i: (0, 0)),
                  pl.BlockSpec((d, n), lambda i: (0, 0))],
        out_specs=[pl.BlockSpec((tm, n_ahy), lambda i: (i, 0)),
                   pl.BlockSpec((tm, n - n_ahy), lambda i: (i, 0))],
        out_shape=[jax.ShapeDtypeStruct((t, n_ahy), F32), jax.ShapeDtypeStruct((t, n - n_ahy), F32)],
        compiler_params=_params(("parallel",), 56),
        name="inproj",
    )(x1, x2, g.reshape(1, d), w_bf)


def _local_kernel(ahy_ref, prev_ref, next_ref, sga_ref, pw_ref, ps_ref, cw_ref, cb_ref, wpo_ref,
                  p1_ref, x0_ref, z_ref, zb_ref, *, seq_len, d_pool, d_hy):
    tm = ahy_ref.shape[0]
    ext_rows = tm + 2 * POOL_HALO
    tiles_per_seq = seq_len // tm
    j = pl.program_id(0) % tiles_per_seq
    prev = jnp.where(j == 0, 0.0, prev_ref[...])
    nxt = jnp.where(j == tiles_per_seq - 1, 0.0, next_ref[...])
    pos = j * tm + lax.broadcasted_iota(I32, (tm, 1), 0)

    def ext(lo, hi):
        return jnp.concatenate([prev[:, lo:hi], ahy_ref[:, lo:hi], nxt[:, lo:hi]], axis=0)

    def shifted(v, k):
        return pltpu.roll(v, k % ext_rows, axis=0)

    group = d_pool // len(POOL_WINDOWS)
    mixed = []
    for g, w in enumerate(POOL_WINDOWS):
        half = w // 2
        e = ext(g * group, (g + 1) * group)
        win = e + shifted(e, 1)
        span = 1
        while span < half:
            win = shifted(win, span) + shifted(win, -span)
            span *= 2
        win = win[POOL_HALO:POOL_HALO + tm]
        cnt = (jnp.minimum(pos + half, seq_len) - jnp.maximum(pos - half, 0)).astype(F32)
        pooled = win / cnt - ahy_ref[:, g * group:(g + 1) * group]
        mixed.append(_dot(pooled.astype(BF16), pw_ref[g]))
    pm = jnp.concatenate(mixed, axis=1) * ps_ref[...]
    p1_ref[...] = sga_ref[...] * _dot(pm.astype(BF16), wpo_ref[...])

    d_br = d_hy // 3
    branches = []
    for b in range(3):
        lo, hi = d_pool + b * d_br, d_pool + (b + 1) * d_br
        e = ext(lo, hi)
        c0, c1 = b * d_br, (b + 1) * d_br
        uc = (shifted(e, 1)[POOL_HALO:POOL_HALO + tm] * cw_ref[0:1, c0:c1]
              + ahy_ref[:, lo:hi] * cw_ref[1:2, c0:c1]
              + shifted(e, -1)[POOL_HALO:POOL_HALO + tm] * cw_ref[2:3, c0:c1]
              + cb_ref[:, c0:c1])
        branches.append(uc)
    x0, x1, v = branches
    z = v * x1
    x0_ref[...] = x0
    z_ref[...] = z
    zb_ref[...] = z.astype(BF16)


def _local(ahy, sg, pool_w_bf, pool_scale, conv_w, conv_b, w_pool_out_bf, seq_len, d_pool, d_hy, tm):
    t, n_ahy = ahy.shape
    d_model = w_pool_out_bf.shape[1]
    d_br = d_hy // 3
    hb = tm // POOL_HALO
    n_halo_blocks = t // POOL_HALO
    return pl.pallas_call(
        functools.partial(_local_kernel, seq_len=seq_len, d_pool=d_pool, d_hy=d_hy),
        grid=(t // tm,),
        in_specs=[pl.BlockSpec((tm, n_ahy), lambda i: (i, 0)),
                  pl.BlockSpec((POOL_HALO, n_ahy), lambda i: (jnp.maximum(i * hb - 1, 0), 0)),
                  pl.BlockSpec((POOL_HALO, n_ahy), lambda i: (jnp.minimum((i + 1) * hb, n_halo_blocks - 1), 0)),
                  pl.BlockSpec((tm, d_model), lambda i: (i, 0)),
                  pl.BlockSpec(pool_w_bf.shape, lambda i: (0, 0, 0)),
                  pl.BlockSpec((1, d_pool), lambda i: (0, 0)),
                  pl.BlockSpec(conv_w.shape, lambda i: (0, 0)),
                  pl.BlockSpec((1, d_hy), lambda i: (0, 0)),
                  pl.BlockSpec(w_pool_out_bf.shape, lambda i: (0, 0))],
        out_specs=[pl.BlockSpec((tm, d_model), lambda i: (i, 0)),
                   pl.BlockSpec((tm, d_br), lambda i: (i, 0)),
                   pl.BlockSpec((tm, d_br), lambda i: (i, 0)),
                   pl.BlockSpec((tm, d_br), lambda i: (i, 0))],
        out_shape=[jax.ShapeDtypeStruct((t, d_model), F32),
                   jax.ShapeDtypeStruct((t, d_br), F32),
                   jax.ShapeDtypeStruct((t, d_br), F32),
                   jax.ShapeDtypeStruct((t, d_br), BF16)],
        compiler_params=_params(("parallel",), 56),
        name="local_mix",
    )(ahy, ahy, ahy, sg, pool_w_bf, pool_scale.reshape(1, d_pool), conv_w, conv_b.reshape(1, d_hy), w_pool_out_bf)


def _filter_kernel(freq_ref, w1_ref, b1_ref, f1_ref, w2_ref, b2_ref, f2_ref, w3_ref, dl_ref, ks_ref, kd_ref,
                   *, seq_len):
    tl = ks_ref.shape[0]
    d_h = ks_ref.shape[1]
    i = (pl.program_id(0) * tl + lax.broadcasted_iota(I32, (tl, 1), 0)).astype(F32)
    t = i / (seq_len - 1.0)
    wpos = (2.0 * math.pi) * i / seq_len
    lane = lax.broadcasted_iota(I32, (tl, LANES), 1)
    bands = (FILTER_EMB - 1) // 2
    ang = freq_ref[...] * wpos
    feat = jnp.where(lane == 0, t,
                     jnp.where(lane <= bands, jnp.cos(ang),
                               jnp.where(lane <= 2 * bands, -jnp.sin(ang), 0.0)))
    h = jnp.sin(f1_ref[...] * (_dot_f32(feat, w1_ref[...]) + b1_ref[...]))
    h = jnp.sin(f2_ref[...] * (_dot_f32(h, w2_ref[...]) + b2_ref[...]))
    h = _dot_f32(h, w3_ref[...])
    decay = jnp.exp(-t * jnp.abs(dl_ref[...]))
    kf = h[:, :d_h] * decay
    kb = h[:, d_h:] * decay
    ks_ref[...] = kf + kb
    kd_ref[...] = kf - kb


def _filters(w1, b1, f1, w2, b2, f2, w3, seq_len, d_h):
    hid = w1.shape[1]
    bands = (FILTER_EMB - 1) // 2
    freqs = np.zeros((1, LANES), np.float32)
    fr = np.linspace(1e-4, bands - 1, bands, dtype=np.float32)
    freqs[0, 1:1 + bands] = fr
    freqs[0, 1 + bands:1 + 2 * bands] = fr
    w1p = jnp.zeros((LANES, hid), F32).at[:FILTER_EMB].set(w1)
    max_decay = math.log(DECAY_TARGET) / FAST_DECAY_PCT
    min_decay = math.log(DECAY_TARGET) / SLOW_DECAY_PCT
    deltas = np.linspace(min_decay, max_decay, d_h, dtype=np.float32).reshape(1, d_h)
    tl = min(512, seq_len)
    full = lambda shape: pl.BlockSpec(shape, lambda i: tuple(0 for _ in shape))
    return pl.pallas_call(
        functools.partial(_filter_kernel, seq_len=seq_len),
        grid=(seq_len // tl,),
        in_specs=[full((1, LANES)), full((LANES, hid)), full((1, hid)), full((1, hid)),
                  full((hid, hid)), full((1, hid)), full((1, hid)), full((hid, 2 * d_h)), full((1, d_h))],
        out_specs=[pl.BlockSpec((tl, d_h), lambda i: (i, 0)), pl.BlockSpec((tl, d_h), lambda i: (i, 0))],
        out_shape=[jax.ShapeDtypeStruct((seq_len, d_h), F32), jax.ShapeDtypeStruct((seq_len, d_h), F32)],
        compiler_params=_params(("parallel",), 32),
        name="hyena_filters",
    )(jnp.asarray(freqs), w1p, b1.reshape(1, hid), f1.reshape(1, hid), w2, b2.reshape(1, hid),
      f2.reshape(1, hid), w3, jnp.asarray(deltas))


def _dftgen_kernel(c_ref, s_ref, st_ref, *, n_fft):
    tr, n = c_ref.shape
    row = pl.program_id(0) * tr + lax.broadcasted_iota(I32, (tr, n), 0)
    col = lax.broadcasted_iota(I32, (tr, n), 1)
    ang = ((row * col) & (n_fft - 1)).astype(F32) * (2.0 * math.pi / n_fft)
    c_ref[...] = jnp.cos(ang).astype(BF16)
    ms = -jnp.sin(ang)
    sign_col = (1 - 2 * (col & 1)).astype(F32)
    sign_row = (1 - 2 * (row & 1)).astype(F32)
    s_ref[...] = jnp.where(row == 0, sign_col, ms).astype(BF16)
    st_ref[...] = jnp.where(col == 0, sign_row, ms).astype(BF16)


def _dftgen(seq_len):
    tr = min(256, seq_len)
    spec = pl.BlockSpec((tr, seq_len), lambda i: (i, 0))
    shp = jax.ShapeDtypeStruct((seq_len, seq_len), BF16)
    return pl.pallas_call(
        functools.partial(_dftgen_kernel, n_fft=2 * seq_len),
        grid=(seq_len // tr,),
        out_specs=[spec, spec, spec],
        out_shape=[shp, shp, shp],
        compiler_params=_params(("parallel",), 48),
        name="dft_matrices",
    )()


def _spectrum_kernel(c_ref, s_ref, ks_ref, kd_ref, kr_ref, ki_ref):
    ks = ks_ref[...]
    kr_ref[...] = _dot(c_ref[...], ks.astype(BF16))
    ki = _dot(s_ref[...], kd_ref[...].astype(BF16))
    tf = ki.shape[0]
    row = lax.broadcasted_iota(I32, (tf, 1), 0)
    sign = (1 - 2 * (lax.broadcasted_iota(I32, (ks.shape[0], 1), 0) & 1)).astype(F32)
    nyq = jnp.sum(ks * sign, axis=0, keepdims=True)
    first = jnp.logical_and(pl.program_id(0) == 0, row == 0)
    ki_ref[...] = jnp.where(first, nyq, ki)


def _spectrum(cmat, smat, ks, kd, tf):
    n, d_h = ks.shape
    return pl.pallas_call(
        _spectrum_kernel,
        grid=(n // tf,),
        in_specs=[pl.BlockSpec((tf, n), lambda i: (i, 0)), pl.BlockSpec((tf, n), lambda i: (i, 0)),
                  pl.BlockSpec((n, d_h), lambda i: (0, 0)), pl.BlockSpec((n, d_h), lambda i: (0, 0))],
        out_specs=[pl.BlockSpec((tf, d_h), lambda i: (i, 0)), pl.BlockSpec((tf, d_h), lambda i: (i, 0))],
        out_shape=[jax.ShapeDtypeStruct((n, d_h), F32), jax.ShapeDtypeStruct((n, d_h), F32)],
        compiler_params=_params(("parallel",), 56),
        name="filter_spectrum",
    )(cmat, smat, ks, kd)


def _fwd_kernel(c_ref, s_ref, z_ref, kr_ref, ki_ref, yr_ref, yi_ref, *, n_fft):
    z = z_ref[...]
    zr = _dot(c_ref[...], z)
    zi = _dot(s_ref[...], z)
    kr = kr_ref[...]
    ki = ki_ref[...]
    row = lax.broadcasted_iota(I32, (zr.shape[0], 1), 0)
    first = jnp.logical_and(pl.program_id(0) == 0, row == 0)
    yr = jnp.where(first, zr * kr, zr * kr - zi * ki)
    yi = jnp.where(first, zi * ki, zr * ki + zi * kr)
    scale = jnp.where(first, 1.0 / n_fft, 2.0 / n_fft)
    yr_ref[...] = (yr * scale).astype(BF16)
    yi_ref[...] = (yi * scale).astype(BF16)


def _inv_kernel(c_ref, st_ref, yr_ref, yi_ref, z_ref, x0_ref, bias_ref, o_ref):
    y = _dot(c_ref[...], yr_ref[...]) + _dot(st_ref[...], yi_ref[...])
    o_ref[...] = ((y + z_ref[...] * bias_ref[...]) * x0_ref[...]).astype(BF16)


def _long_conv(cmat, smat, stmat, zb, z, x0, kr, ki, bias, seq_len, tf):
    t, d_h = z.shape
    nb = t // seq_len
    nf = seq_len // tf
    mat_spec = pl.BlockSpec((tf, seq_len), lambda f, b: (f, 0))
    seq_spec = pl.BlockSpec((seq_len, d_h), lambda f, b: (b, 0))
    tile_spec = pl.BlockSpec((tf, d_h), lambda f, b: (b * nf + f, 0))
    k_spec = pl.BlockSpec((tf, d_h), lambda f, b: (f, 0))
    yr, yi = pl.pallas_call(
        functools.partial(_fwd_kernel, n_fft=2 * seq_len),
        grid=(nf, nb),
        in_specs=[mat_spec, mat_spec, seq_spec, k_spec, k_spec],
        out_specs=[tile_spec, tile_spec],
        out_shape=[jax.ShapeDtypeStruct((t, d_h), BF16), jax.ShapeDtypeStruct((t, d_h), BF16)],
        compiler_params=_params(("parallel", "parallel"), 56),
        name="conv_fwd_dft",
    )(cmat, smat, zb, kr, ki)
    return pl.pallas_call(
        _inv_kernel,
        grid=(nf, nb),
        in_specs=[mat_spec, mat_spec, seq_spec, seq_spec, tile_spec, tile_spec,
                  pl.BlockSpec((1, d_h), lambda f, b: (0, 0))],
        out_specs=tile_spec,
        out_shape=jax.ShapeDtypeStruct((t, d_h), BF16),
        compiler_params=_params(("parallel", "parallel"), 56),
        name="conv_inv_dft",
    )(cmat, stmat, yr, yi, z, x0, bias.reshape(1, d_h))


def _top16(s, payload=None):
    n_rows, n = s.shape
    groups = n_rows // SUBLANES
    slabs = [s[g * SUBLANES:(g + 1) * SUBLANES] for g in range(groups)]
    sub = lax.broadcasted_iota(I32, (SUBLANES, n), 0).astype(F32)
    rid = [sub + float(g * SUBLANES) for g in range(groups)]
    vals, ids = [], []
    for _ in range(PEER_TOPK):
        m = jnp.max(functools.reduce(jnp.maximum, slabs), axis=0, keepdims=True)
        first = jnp.full((SUBLANES, n), float(groups), F32)
        for g in reversed(range(groups)):
            first = jnp.where(slabs[g] == m, float(g), first)
        row = jnp.where(first < float(groups), first * float(SUBLANES) + sub, float(n_rows))
        pos = jnp.min(row, axis=0, keepdims=True)
        hits = [r == pos for r in rid]
        vals.append(m)
        if payload is None:
            ids.append(pos)
        else:
            picked = [jnp.where(hits[g], payload[g * SUBLANES:(g + 1) * SUBLANES], -1.0) for g in range(groups)]
            ids.append(jnp.max(functools.reduce(jnp.maximum, picked), axis=0, keepdims=True))
        slabs = [jnp.where(h, -jnp.inf, sl) for h, sl in zip(hits, slabs)]
    return jnp.concatenate(vals, axis=0), jnp.concatenate(ids, axis=0)


def _route_kernel(hyb_ref, p1_ref, sgb_ref, x1_ref, x2_ref, who_ref, wo_ref, g2_ref, wq_ref, keys_ref,
                  h_ref, hn_ref, idx_ref, gate_ref, q_scr, idx_scr, gate_scr, cand_scr, cidx_scr,
                  *, idx_scale, n_first):
    yb = _dot(hyb_ref[...], who_ref[...])
    mixed = p1_ref[...] + sgb_ref[...] * yb
    h = _two_part_block(x1_ref, x2_ref, n_first) + _dot(mixed.astype(BF16), wo_ref[...])
    hn = _rms(h, g2_ref[...])
    for c in range(h.shape[1] // LANES):
        h_ref[:, c, :] = h[:, c * LANES:(c + 1) * LANES]
        hn_ref[:, c, :] = hn[:, c * LANES:(c + 1) * LANES]
    q = _dot(hn.astype(BF16), wq_ref[...])
    n_half = 2 * PEER_HEADS
    dh = q.shape[1] // n_half
    for j in range(n_half):
        q_scr[j] = q[:, j * dh:(j + 1) * dh].astype(BF16)

    def head(hd, carry):
        sv0, si0 = _top16(_dot_nt(keys_ref[2 * hd], q_scr[2 * hd]))
        sv1, si1 = _top16(_dot_nt(keys_ref[2 * hd + 1], q_scr[2 * hd + 1]))
        cand_scr[...] = jnp.full(cand_scr.shape, -jnp.inf, F32)
        cidx_scr[...] = jnp.full(cidx_scr.shape, -1.0, F32)
        off = 0
        for a in range(PEER_TOPK):
            nb = PEER_TOPK // (a + 1)
            cand_scr[off:off + nb, :] = sv0[a:a + 1] + sv1[:nb]
            cidx_scr[off:off + nb, :] = (si0[a:a + 1] * float(PEER_NKEYS) + si1[:nb]) * float(idx_scale)
            off += nb
        best, eid = _top16(cand_scr[...], cidx_scr[...])
        ex = jnp.exp(best - best[0:1])
        r0 = pl.multiple_of(hd * PEER_TOPK, PEER_TOPK)
        idx_scr[pl.ds(r0, PEER_TOPK), :] = eid.astype(I32)
        gate_scr[pl.ds(r0, PEER_TOPK), :] = ex / jnp.sum(ex, axis=0, keepdims=True)
        return carry

    lax.fori_loop(0, PEER_HEADS, head, 0)
    idx_ref[...] = idx_scr[...].T
    gate_ref[...] = gate_scr[...].T


def _route(hyb, p1, sg, x1, x2, who_bf, wo_bf, g2, wq_bf, keys_bf, tm, idx_scale):
    d = x1.shape[1]
    t = x1.shape[0] + x2.shape[0]
    n_first = x1.shape[0] // tm
    d_h = hyb.shape[1]
    n_half, nk, dh = keys_bf.shape
    n_cand = sum(PEER_TOPK // (a + 1) for a in range(PEER_TOPK))
    n_cand = -(-n_cand // SUBLANES) * SUBLANES
    tok = lambda w: pl.BlockSpec((tm, w), lambda i: (i, 0))
    full = lambda a: pl.BlockSpec(a.shape, lambda i: tuple(0 for _ in a.shape))
    tok3 = pl.BlockSpec((tm, d // LANES, LANES), lambda i: (i, 0, 0))
    return pl.pallas_call(
        functools.partial(_route_kernel, idx_scale=idx_scale, n_first=n_first),
        grid=(t // tm,),
        in_specs=[tok(d_h), tok(d), pl.BlockSpec((tm, d), lambda i: (i, 1))] + _two_part_specs(tm, d, n_first) + [
                  full(who_bf), full(wo_bf), pl.BlockSpec((1, d), lambda i: (0, 0)), full(wq_bf), full(keys_bf)],
        out_specs=[tok3, tok3, tok(PEER_SLOTS), tok(PEER_SLOTS)],
        out_shape=[jax.ShapeDtypeStruct((t, d // LANES, LANES), F32), jax.ShapeDtypeStruct((t, d // LANES, LANES), F32),
                   jax.ShapeDtypeStruct((t, PEER_SLOTS), I32), jax.ShapeDtypeStruct((t, PEER_SLOTS), F32)],
        scratch_shapes=[pltpu.VMEM((n_half, tm, dh), BF16),
                        pltpu.VMEM((PEER_SLOTS, tm), I32),
                        pltpu.VMEM((PEER_SLOTS, tm), F32),
                        pltpu.VMEM((n_cand, tm), F32),
                        pltpu.VMEM((n_cand, tm), F32)],
        compiler_params=_params(("parallel",), 56),
        name="route",
    )(hyb, p1, sg, x1, x2, who_bf, wo_bf, g2.reshape(1, d), wq_bf, keys_bf)


def _slot_pair(tab_ref, ids, j, e_a, e_b):
    rows = tab_ref.shape[0] // (PEER_NKEYS * PEER_NKEYS)
    assert 2 * rows == SUBLANES
    lo = tab_ref[pl.ds(pl.multiple_of(ids[j, e_a], rows), rows), :]
    hi = tab_ref[pl.ds(pl.multiple_of(ids[j, e_b], rows), rows), :]
    return jnp.concatenate([lo, hi], axis=0)


def _split2(x):
    hi = x.astype(BF16)
    return hi, (x - hi.astype(F32)).astype(BF16)


def _gelu(x):
    return 0.5 * x * (1.0 + jnp.tanh(math.sqrt(2.0 / math.pi) * (x + 0.044715 * (x * x * x))))


ID_CHUNK = 32
V_SLOTS_PER_DOT = 32


def _id_copy(idx_hbm, chunk, dst, sem):
    return pltpu.make_async_copy(idx_hbm.at[pl.ds(chunk * ID_CHUNK, ID_CHUNK), :], dst, sem)


def _token_pipeline(idx_hbm, id_bufs, sems, tb, region):
    chunks_per_step = tb // ID_CHUNK
    assert chunks_per_step % 2 == 0
    first = pl.program_id(0) * chunks_per_step
    total = pl.num_programs(0) * chunks_per_step

    def two_chunks(k, carry):
        c0 = first + 2 * k

        @pl.when(c0 == 0)
        def _():
            _id_copy(idx_hbm, c0, id_bufs[0], sems.at[0]).start()

        _id_copy(idx_hbm, c0 + 1, id_bufs[1], sems.at[1]).start()
        _id_copy(idx_hbm, c0, id_bufs[0], sems.at[0]).wait()
        region(id_bufs[0], 2 * k * ID_CHUNK)

        @pl.when(c0 + 2 < total)
        def _():
            _id_copy(idx_hbm, c0 + 2, id_bufs[0], sems.at[0]).start()

        _id_copy(idx_hbm, c0 + 1, id_bufs[1], sems.at[1]).wait()
        region(id_bufs[1], (2 * k + 1) * ID_CHUNK)
        return carry

    lax.fori_loop(0, chunks_per_step // 2, two_chunks, 0)


def _id_scratch():
    return [pltpu.SMEM((ID_CHUNK, PEER_SLOTS), I32), pltpu.SMEM((ID_CHUNK, PEER_SLOTS), I32),
            pltpu.SemaphoreType.DMA((2,))]


U_GROUP_ORDER = (0, 2, 1, 3)


def _peer_u_kernel(idx_hbm, x_ref, gate_ref, rep_ref, tab_ref, arep_ref, pre_ref, id_a, id_b, sems):
    tb = x_ref.shape[0]
    rows = tab_ref.shape[0] // (PEER_NKEYS * PEER_NKEYS)
    assert rows == 4, "the sublane butterfly below is written for 4 packed rows per expert"
    ones = jnp.ones((SUBLANES, LANES), BF16)
    sub = lax.broadcasted_iota(I32, (SUBLANES, LANES), 0)
    low_pair = (sub % 4) < 2
    even_row = (sub % 2) == 0

    def compute(t, ids, jt):
        x_lo = jnp.concatenate([x_ref[t, pl.ds(0, rows, stride=2), :]] * 2, axis=0)
        x_hi = jnp.concatenate([x_ref[t, pl.ds(1, rows, stride=2), :]] * 2, axis=0)
        merged = []
        for m in range(PEER_SLOTS // SUBLANES):
            a = []
            for j in range(4):
                e = SUBLANES * m + U_GROUP_ORDER[j]
                w = _slot_pair(tab_ref, ids, jt, e, e + 4)
                p = pltpu.bitcast(w << 16, F32) * x_lo + pltpu.bitcast(w & HI_MASK, F32) * x_hi
                a.append(p + pltpu.roll(p, SUBLANES - 2, axis=0))
            ab = jnp.where(low_pair, a[0], pltpu.roll(a[1], 2, axis=0))
            cd = jnp.where(low_pair, a[2], pltpu.roll(a[3], 2, axis=0))
            ab = ab + pltpu.roll(ab, SUBLANES - 1, axis=0)
            cd = cd + pltpu.roll(cd, SUBLANES - 1, axis=0)
            merged.append(jnp.where(even_row, ab, pltpu.roll(cd, 1, axis=0)))
        p_hi, p_lo = _split2(jnp.concatenate(merged, axis=0))
        sums = _dot_nt(ones, p_hi) + _dot_nt(ones, p_lo)
        pre_ref[pl.ds(t, 1), :] = sums[0:1]

    def region(ids, t_base):
        for j in range(ID_CHUNK):
            compute(t_base + j, ids, j)

    _token_pipeline(idx_hbm, (id_a, id_b), sems, tb, region)
    act = _gelu(pre_ref[...]) * gate_ref[...]
    a1 = act.astype(BF16)
    r1 = act - a1.astype(F32)
    a2, a3 = _split2(r1)
    rep = rep_ref[...]
    arep_ref[...] = _dot(a1, rep) + _dot(a2, rep) + _dot(a3, rep)


def _peer_u(idx, x3, gate, tab, tb):
    t, nch, _ = x3.shape
    rep = np.zeros((PEER_SLOTS, PEER_SLOTS * nch), np.float32)
    for e in range(PEER_SLOTS):
        rep[e, e * nch:(e + 1) * nch] = 1.0
    return pl.pallas_call(
        _peer_u_kernel,
        grid=(t // tb,),
        in_specs=[pl.BlockSpec(memory_space=pl.ANY),
                  pl.BlockSpec((tb, nch, LANES), lambda i: (i, 0, 0)),
                  pl.BlockSpec((tb, PEER_SLOTS), lambda i: (i, 0)),
                  pl.BlockSpec(rep.shape, lambda i: (0, 0)),
                  pl.BlockSpec(tab.shape, lambda i: (0, 0), pipeline_mode=pl.Buffered(1))],
        out_specs=pl.BlockSpec((tb, PEER_SLOTS * nch), lambda i: (i, 0)),
        out_shape=jax.ShapeDtypeStruct((t, PEER_SLOTS * nch), F32),
        scratch_shapes=[pltpu.VMEM((tb, PEER_SLOTS), F32)] + _id_scratch(),
        compiler_params=_params(("arbitrary",), 48),
        name="peer_u",
    )(idx, x3, gate, jnp.asarray(rep, BF16), tab)


def _peer_v_kernel(idx_hbm, arep_ref, h_ref, g_ref, tab_ref, o1_ref, o2_ref, y_ref, id_a, id_b, sems, *, n_first):
    tb, nch, _ = h_ref.shape
    shape = (nch, arep_ref.shape[1])
    own_chunk = (lax.broadcasted_iota(I32, shape, 1) % nch) == lax.broadcasted_iota(I32, shape, 0)

    def compute(t, ids, j):
        w_hi, w_lo = _split2(jnp.where(own_chunk, arep_ref[pl.ds(t, 1), :], 0.0))
        acc = None
        for e0 in range(0, PEER_SLOTS, V_SLOTS_PER_DOT):
            pieces = [_slot_pair(tab_ref, ids, j, e, e + 1) for e in range(e0, e0 + V_SLOTS_PER_DOT, 2)]
            vals = pltpu.bitcast(jnp.concatenate(pieces, axis=0), BF16)
            k0, k1 = e0 * nch, (e0 + V_SLOTS_PER_DOT) * nch
            part = _dot(w_hi[:, k0:k1], vals) + _dot(w_lo[:, k0:k1], vals)
            acc = part if acc is None else acc + part
        y_ref[t] = acc

    def region(ids, t_base):
        for j in range(ID_CHUNK):
            compute(t_base + j, ids, j)

    _token_pipeline(idx_hbm, (id_a, id_b), sems, tb, region)
    y = h_ref[...] + y_ref[...]
    ms = jnp.sum(jnp.sum(y * y, axis=2, keepdims=True), axis=1, keepdims=True) * (1.0 / (y.shape[1] * y.shape[2]))
    y_ref[...] = y * lax.rsqrt(ms + EPS) * g_ref[...]

    def emit(o_ref):
        for c in range(nch):
            o_ref[:, c * LANES:(c + 1) * LANES] = y_ref[:, c, :]

    @pl.when(pl.program_id(0) < n_first)
    def _():
        emit(o1_ref)

    @pl.when(pl.program_id(0) >= n_first)
    def _():
        emit(o2_ref)


def _peer_v(idx, arep, h3, g, tab, tb, t_first):
    t, nch, _ = h3.shape
    d = nch * LANES
    n_first = t_first // tb
    blk = pl.BlockSpec((tb, nch, LANES), lambda i: (i, 0, 0))
    return pl.pallas_call(
        functools.partial(_peer_v_kernel, n_first=n_first),
        grid=(t // tb,),
        in_specs=[pl.BlockSpec(memory_space=pl.ANY),
                  pl.BlockSpec((tb, arep.shape[1]), lambda i: (i, 0)),
                  blk,
                  pl.BlockSpec((1, nch, LANES), lambda i: (0, 0, 0)),
                  pl.BlockSpec(tab.shape, lambda i: (0, 0), pipeline_mode=pl.Buffered(1))],
        out_specs=[pl.BlockSpec((tb, d), lambda i: (jnp.minimum(i, n_first - 1), 0)),
                   pl.BlockSpec((tb, d), lambda i: (jnp.maximum(i - n_first, 0), 0))],
        out_shape=[jax.ShapeDtypeStruct((t_first, d), F32), jax.ShapeDtypeStruct((t - t_first, d), F32)],
        scratch_shapes=[pltpu.VMEM((tb, nch, LANES), F32)] + _id_scratch(),
        compiler_params=_params(("arbitrary",), 48),
        name="peer_v",
    )(idx, arep, h3, g.reshape(1, nch, LANES), tab)


def _pick_tile(n, want):
    while n % want:
        want //= 2
    return want


def kernel(x_prompt, x_sample, norm1_g, w_in, pool_w, pool_scale, conv_w, conv_b, filt_w1, filt_b1, filt_f1, filt_w2, filt_b2, filt_f2, filt_w3, hyena_bias, w_pool_out, w_hyena_out, w_o, norm2_g, peer_wq, peer_keys, peer_u, peer_v, normf_g):
    assert norm1_g.shape[0] == 1, "single-layer block"
    assert x_prompt.shape[1:] == x_sample.shape[1:]
    nb_p, seq_len, d = x_prompt.shape
    d_pool = pool_scale.shape[-1]
    d_hy = conv_b.shape[-1]
    d_h = d_hy // 3
    x1 = x_prompt.reshape(-1, d)
    x2 = x_sample.reshape(-1, d)
    tm = _pick_tile(seq_len, 512)

    ahy, sg = _inproj(x1, x2, norm1_g[0], w_in[0].astype(BF16), d_pool + d_hy, tm)
    p1, x0, z, zb = _local(ahy, sg, pool_w[0].astype(BF16), pool_scale[0], conv_w[0], conv_b[0],
                           w_pool_out[0].astype(BF16), seq_len, d_pool, d_hy, tm)
    ks, kd = _filters(filt_w1[0], filt_b1[0], filt_f1[0], filt_w2[0], filt_b2[0], filt_f2[0], filt_w3[0],
                      seq_len, d_h)
    cmat, smat, stmat = _dftgen(seq_len)
    kr, ki = _spectrum(cmat, smat, ks, kd, tm)
    hyb = _long_conv(cmat, smat, stmat, zb, z, x0, kr, ki, hyena_bias[0], seq_len, tm)

    keys = peer_keys[0]
    keys_bf = keys.reshape((-1,) + keys.shape[2:]).astype(BF16)
    h, hn, idx, gate = _route(hyb, p1, sg, x1, x2, w_hyena_out[0].astype(BF16), w_o[0].astype(BF16), norm2_g[0],
                              peer_wq[0].astype(BF16), keys_bf, _pick_tile(seq_len, 512),
                              idx_scale=d // (2 * LANES))

    tb = _pick_tile(seq_len, 256)
    arep = _peer_u(idx, hn, gate, _pack_table(peer_u[0]), tb)
    y_p, y_s = _peer_v(idx, arep, h, normf_g, _pack_table(peer_v[0]), tb,
                       nb_p * seq_len)
    return (y_p.reshape(x_prompt.shape), y_s.reshape(x_sample.shape))
```

```python
import functools
import math

import jax
import jax.numpy as jnp
import numpy as np
from jax import lax
from jax.experimental import pallas as pl
from jax.experimental.pallas import tpu as pltpu

F32 = jnp.float32
BF16 = jnp.bfloat16
I32 = jnp.int32

EPS = 1e-6
POOL_WINDOWS = (2, 4, 8, 16)
POOL_HALO = 8
PEER_HEADS = 8
PEER_NKEYS = 128
PEER_TOPK = 16
PEER_SLOTS = PEER_HEADS * PEER_TOPK
FILTER_EMB = 33
DECAY_TARGET = 1e-2
FAST_DECAY_PCT = 0.3
SLOW_DECAY_PCT = 1.5
LANES = 128
SUBLANES = 8
HI_MASK = -65536
MIB = 1024 * 1024


def _params(sem, vmem_mib):
    return pltpu.CompilerParams(dimension_semantics=sem, vmem_limit_bytes=vmem_mib * MIB)


def _dot(a, b):
    return jnp.dot(a, b, preferred_element_type=F32)


def _dot_nt(a, b):
    return lax.dot_general(a, b, (((1,), (1,)), ((), ())), preferred_element_type=F32)


def _dot_f32(a, b):
    return jnp.dot(a, b, preferred_element_type=F32, precision=lax.Precision.HIGHEST)


def _rms(x, g):
    return x * lax.rsqrt(jnp.mean(x * x, axis=-1, keepdims=True) + EPS) * g


def _pack_kernel(t_ref, o_ref):
    x = t_ref[...]
    for r in range(x.shape[1] // (2 * LANES)):
        lo = pltpu.bitcast(x[:, (2 * r) * LANES:(2 * r + 1) * LANES].astype(BF16).astype(F32), I32)
        hi = pltpu.bitcast(x[:, (2 * r + 1) * LANES:(2 * r + 2) * LANES].astype(BF16).astype(F32), I32)
        o_ref[:, r * LANES:(r + 1) * LANES] = hi | lax.shift_right_logical(lo, 16)


def _pack_table(tab):
    e, d = tab.shape
    rows = 512
    packed = pl.pallas_call(
        _pack_kernel,
        grid=(e // rows,),
        in_specs=[pl.BlockSpec((rows, d), lambda i: (i, 0))],
        out_specs=pl.BlockSpec((rows, d // 2), lambda i: (i, 0)),
        out_shape=jax.ShapeDtypeStruct((e, d // 2), I32),
        compiler_params=_params(("parallel",), 32),
        name="pack_table",
    )(tab)
    return packed.reshape(e * (d // 2) // LANES, LANES)


def _two_part_specs(tm, d, n_first):
    return [pl.BlockSpec((tm, d), lambda i: (jnp.minimum(i, n_first - 1), 0)),
            pl.BlockSpec((tm, d), lambda i: (jnp.maximum(i - n_first, 0), 0))]


def _two_part_block(x1_ref, x2_ref, n_first):
    return jnp.where(pl.program_id(0) < n_first, x1_ref[...], x2_ref[...])


def _inproj_kernel(x1_ref, x2_ref, g_ref, w_ref, ahy_ref, sg_ref, *, n_ahy, chunk, n_first):
    xn = _rms(_two_part_block(x1_ref, x2_ref, n_first), g_ref[...]).astype(BF16)
    for j in range(n_ahy // chunk):
        ahy_ref[:, j * chunk:(j + 1) * chunk] = _dot(xn, w_ref[:, j * chunk:(j + 1) * chunk])
    n_g = w_ref.shape[1] - n_ahy
    for j in range(n_g // chunk):
        g = _dot(xn, w_ref[:, n_ahy + j * chunk:n_ahy + (j + 1) * chunk])
        sg_ref[:, j * chunk:(j + 1) * chunk] = 1.0 / (1.0 + jnp.exp(-g))


def _inproj(x1, x2, g, w_bf, n_ahy, tm):
    d = x1.shape[1]
    t = x1.shape[0] + x2.shape[0]
    n = w_bf.shape[1]
    n_first = x1.shape[0] // tm
    return pl.pallas_call(
        functools.partial(_inproj_kernel, n_ahy=n_ahy, chunk=min(1024, n_ahy, n - n_ahy), n_first=n_first),
        grid=(t // tm,),
        in_specs=_two_part_specs(tm, d, n_first) + [
                  pl.BlockSpec((1, d), lambda i: (0, 0)),
                  pl.BlockSpec((d, n), lambda i: (0, 0))],
        out_specs=[pl.BlockSpec((tm, n_ahy), lambda i: (i, 0)),
                   pl.BlockSpec((tm, n - n_ahy), lambda i: (i, 0))],
        out_shape=[jax.ShapeDtypeStruct((t, n_ahy), F32), jax.ShapeDtypeStruct((t, n - n_ahy), F32)],
        compiler_params=_params(("parallel",), 56),
        name="inproj",
    )(x1, x2, g.reshape(1, d), w_bf)


def _local_kernel(ahy_ref, prev_ref, next_ref, sga_ref, pw_ref, ps_ref, cw_ref, cb_ref, wpo_ref,
                  p1_ref, x0_ref, z_ref, zb_ref, *, seq_len, d_pool, d_hy):
    tm = ahy_ref.shape[0]
    ext_rows = tm + 2 * POOL_HALO
    tiles_per_seq = seq_len // tm
    j = pl.program_id(0) % tiles_per_seq
    prev = jnp.where(j == 0, 0.0, prev_ref[...])
    nxt = jnp.where(j == tiles_per_seq - 1, 0.0, next_ref[...])
    pos = j * tm + lax.broadcasted_iota(I32, (tm, 1), 0)

    def ext(lo, hi):
        return jnp.concatenate([prev[:, lo:hi], ahy_ref[:, lo:hi], nxt[:, lo:hi]], axis=0)

    def shifted(v, k):
        return pltpu.roll(v, k % ext_rows, axis=0)

    group = d_pool // len(POOL_WINDOWS)
    mixed = []
    for g, w in enumerate(POOL_WINDOWS):
        half = w // 2
        e = ext(g * group, (g + 1) * group)
        win = e + shifted(e, 1)
        span = 1
        while span < half:
            win = shifted(win, span) + shifted(win, -span)
            span *= 2
        win = win[POOL_HALO:POOL_HALO + tm]
        cnt = (jnp.minimum(pos + half, seq_len) - jnp.maximum(pos - half, 0)).astype(F32)
        pooled = win / cnt - ahy_ref[:, g * group:(g + 1) * group]
        mixed.append(_dot(pooled.astype(BF16), pw_ref[g]))
    pm = jnp.concatenate(mixed, axis=1) * ps_ref[...]
    p1_ref[...] = sga_ref[...] * _dot(pm.astype(BF16), wpo_ref[...])

    d_br = d_hy // 3
    branches = []
    for b in range(3):
        lo, hi = d_pool + b * d_br, d_pool + (b + 1) * d_br
        e = ext(lo, hi)
        c0, c1 = b * d_br, (b + 1) * d_br
        uc = (shifted(e, 1)[POOL_HALO:POOL_HALO + tm] * cw_ref[0:1, c0:c1]
              + ahy_ref[:, lo:hi] * cw_ref[1:2, c0:c1]
              + shifted(e, -1)[POOL_HALO:POOL_HALO + tm] * cw_ref[2:3, c0:c1]
              + cb_ref[:, c0:c1])
        branches.append(uc)
    x0, x1, v = branches
    z = v * x1
    x0_ref[...] = x0
    z_ref[...] = z
    zb_ref[...] = z.astype(BF16)


def _local(ahy, sg, pool_w_bf, pool_scale, conv_w, conv_b, w_pool_out_bf, seq_len, d_pool, d_hy, tm):
    t, n_ahy = ahy.shape
    d_model = w_pool_out_bf.shape[1]
    d_br = d_hy // 3
    hb = tm // POOL_HALO
    n_halo_blocks = t // POOL_HALO
    return pl.pallas_call(
        functools.partial(_local_kernel, seq_len=seq_len, d_pool=d_pool, d_hy=d_hy),
        grid=(t // tm,),
        in_specs=[pl.BlockSpec((tm, n_ahy), lambda i: (i, 0)),
                  pl.BlockSpec((POOL_HALO, n_ahy), lambda i: (jnp.maximum(i * hb - 1, 0), 0)),
                  pl.BlockSpec((POOL_HALO, n_ahy), lambda i: (jnp.minimum((i + 1) * hb, n_halo_blocks - 1), 0)),
                  pl.BlockSpec((tm, d_model), lambda i: (i, 0)),
                  pl.BlockSpec(pool_w_bf.shape, lambda i: (0, 0, 0)),
                  pl.BlockSpec((1, d_pool), lambda i: (0, 0)),
                  pl.BlockSpec(conv_w.shape, lambda i: (0, 0)),
                  pl.BlockSpec((1, d_hy), lambda i: (0, 0)),
                  pl.BlockSpec(w_pool_out_bf.shape, lambda i: (0, 0))],
        out_specs=[pl.BlockSpec((tm, d_model), lambda i: (i, 0)),
                   pl.BlockSpec((tm, d_br), lambda i: (i, 0)),
                   pl.BlockSpec((tm, d_br), lambda i: (i, 0)),
                   pl.BlockSpec((tm, d_br), lambda i: (i, 0))],
        out_shape=[jax.ShapeDtypeStruct((t, d_model), F32),
                   jax.ShapeDtypeStruct((t, d_br), F32),
                   jax.ShapeDtypeStruct((t, d_br), F32),
                   jax.ShapeDtypeStruct((t, d_br), BF16)],
        compiler_params=_params(("parallel",), 56),
        name="local_mix",
    )(ahy, ahy, ahy, sg, pool_w_bf, pool_scale.reshape(1, d_pool), conv_w, conv_b.reshape(1, d_hy), w_pool_out_bf)


def _filter_kernel(freq_ref, w1_ref, b1_ref, f1_ref, w2_ref, b2_ref, f2_ref, w3_ref, dl_ref, ks_ref, kd_ref,
                   *, seq_len):
    tl = ks_ref.shape[0]
    d_h = ks_ref.shape[1]
    i = (pl.program_id(0) * tl + lax.broadcasted_iota(I32, (tl, 1), 0)).astype(F32)
    t = i / (seq_len - 1.0)
    wpos = (2.0 * math.pi) * i / seq_len
    lane = lax.broadcasted_iota(I32, (tl, LANES), 1)
    bands = (FILTER_EMB - 1) // 2
    ang = freq_ref[...] * wpos
    feat = jnp.where(lane == 0, t,
                     jnp.where(lane <= bands, jnp.cos(ang),
                               jnp.where(lane <= 2 * bands, -jnp.sin(ang), 0.0)))
    h = jnp.sin(f1_ref[...] * (_dot_f32(feat, w1_ref[...]) + b1_ref[...]))
    h = jnp.sin(f2_ref[...] * (_dot_f32(h, w2_ref[...]) + b2_ref[...]))
    h = _dot_f32(h, w3_ref[...])
    decay = jnp.exp(-t * jnp.abs(dl_ref[...]))
    kf = h[:, :d_h] * decay
    kb = h[:, d_h:] * decay
    ks_ref[...] = kf + kb
    kd_ref[...] = kf - kb


def _filters(w1, b1, f1, w2, b2, f2, w3, seq_len, d_h):
    hid = w1.shape[1]
    bands = (FILTER_EMB - 1) // 2
    freqs = np.zeros((1, LANES), np.float32)
    fr = np.linspace(1e-4, bands - 1, bands, dtype=np.float32)
    freqs[0, 1:1 + bands] = fr
    freqs[0, 1 + bands:1 + 2 * bands] = fr
    w1p = jnp.zeros((LANES, hid), F32).at[:FILTER_EMB].set(w1)
    max_decay = math.log(DECAY_TARGET) / FAST_DECAY_PCT
    min_decay = math.log(DECAY_TARGET) / SLOW_DECAY_PCT
    deltas = np.linspace(min_decay, max_decay, d_h, dtype=np.float32).reshape(1, d_h)
    tl = min(512, seq_len)
    full = lambda shape: pl.BlockSpec(shape, lambda i: tuple(0 for _ in shape))
    return pl.pallas_call(
        functools.partial(_filter_kernel, seq_len=seq_len),
        grid=(seq_len // tl,),
        in_specs=[full((1, LANES)), full((LANES, hid)), full((1, hid)), full((1, hid)),
                  full((hid, hid)), full((1, hid)), full((1, hid)), full((hid, 2 * d_h)), full((1, d_h))],
        out_specs=[pl.BlockSpec((tl, d_h), lambda i: (i, 0)), pl.BlockSpec((tl, d_h), lambda i: (i, 0))],
        out_shape=[jax.ShapeDtypeStruct((seq_len, d_h), F32), jax.ShapeDtypeStruct((seq_len, d_h), F32)],
        compiler_params=_params(("parallel",), 32),
        name="hyena_filters",
    )(jnp.asarray(freqs), w1p, b1.reshape(1, hid), f1.reshape(1, hid), w2, b2.reshape(1, hid),
      f2.reshape(1, hid), w3, jnp.asarray(deltas))


def _dftgen_kernel(c_ref, s_ref, st_ref, cb_ref, sb_ref, *, n_fft):
    tr, n = c_ref.shape
    theta = 2.0 * math.pi / n_fft
    in_tile = lax.broadcasted_iota(I32, (tr, n), 0)
    col = lax.broadcasted_iota(I32, (tr, n), 1)

    @pl.when(pl.program_id(0) == 0)
    def _():
        ang = ((in_tile * col) & (n_fft - 1)).astype(F32) * theta
        cb_ref[...] = jnp.cos(ang)
        sb_ref[...] = jnp.sin(ang)

    row0 = pl.program_id(0) * tr
    ang0 = ((row0 * lax.broadcasted_iota(I32, (1, n), 1)) & (n_fft - 1)).astype(F32) * theta
    ca, sa = jnp.cos(ang0), jnp.sin(ang0)
    cb, sb = cb_ref[...], sb_ref[...]
    row = row0 + in_tile
    c_ref[...] = (ca * cb - sa * sb).astype(BF16)
    ms = -(sa * cb + ca * sb)
    sign_col = (1 - 2 * (col & 1)).astype(F32)
    sign_row = (1 - 2 * (row & 1)).astype(F32)
    s_ref[...] = jnp.where(row == 0, sign_col, ms).astype(BF16)
    st_ref[...] = jnp.where(col == 0, sign_row, ms).astype(BF16)


def _dftgen(seq_len):
    tr = min(256, seq_len)
    spec = pl.BlockSpec((tr, seq_len), lambda i: (i, 0))
    shp = jax.ShapeDtypeStruct((seq_len, seq_len), BF16)
    return pl.pallas_call(
        functools.partial(_dftgen_kernel, n_fft=2 * seq_len),
        grid=(seq_len // tr,),
        out_specs=[spec, spec, spec],
        out_shape=[shp, shp, shp],
        scratch_shapes=[pltpu.VMEM((tr, seq_len), F32), pltpu.VMEM((tr, seq_len), F32)],
        compiler_params=_params(("arbitrary",), 48),
        name="dft_matrices",
    )()


def _spectrum_kernel(c_ref, s_ref, ks_ref, kd_ref, kr_ref, ki_ref):
    ks = ks_ref[...]
    kr_ref[...] = _dot(c_ref[...], ks.astype(BF16))
    ki = _dot(s_ref[...], kd_ref[...].astype(BF16))
    tf = ki.shape[0]
    row = lax.broadcasted_iota(I32, (tf, 1), 0)
    sign = (1 - 2 * (lax.broadcasted_iota(I32, (ks.shape[0], 1), 0) & 1)).astype(F32)
    nyq = jnp.sum(ks * sign, axis=0, keepdims=True)
    first = jnp.logical_and(pl.program_id(0) == 0, row == 0)
    ki_ref[...] = jnp.where(first, nyq, ki)


def _spectrum(cmat, smat, ks, kd, tf):
    n, d_h = ks.shape
    return pl.pallas_call(
        _spectrum_kernel,
        grid=(n // tf,),
        in_specs=[pl.BlockSpec((tf, n), lambda i: (i, 0)), pl.BlockSpec((tf, n), lambda i: (i, 0)),
                  pl.BlockSpec((n, d_h), lambda i: (0, 0)), pl.BlockSpec((n, d_h), lambda i: (0, 0))],
        out_specs=[pl.BlockSpec((tf, d_h), lambda i: (i, 0)), pl.BlockSpec((tf, d_h), lambda i: (i, 0))],
        out_shape=[jax.ShapeDtypeStruct((n, d_h), F32), jax.ShapeDtypeStruct((n, d_h), F32)],
        compiler_params=_params(("parallel",), 56),
        name="filter_spectrum",
    )(cmat, smat, ks, kd)


def _fwd_kernel(c_ref, s_ref, z_ref, kr_ref, ki_ref, yr_ref, yi_ref, *, n_fft):
    z = z_ref[...]
    zr = _dot(c_ref[...], z)
    zi = _dot(s_ref[...], z)
    kr = kr_ref[...]
    ki = ki_ref[...]
    row = lax.broadcasted_iota(I32, (zr.shape[0], 1), 0)
    first = jnp.logical_and(pl.program_id(0) == 0, row == 0)
    yr = jnp.where(first, zr * kr, zr * kr - zi * ki)
    yi = jnp.where(first, zi * ki, zr * ki + zi * kr)
    scale = jnp.where(first, 1.0 / n_fft, 2.0 / n_fft)
    yr_ref[...] = (yr * scale).astype(BF16)
    yi_ref[...] = (yi * scale).astype(BF16)


def _inv_kernel(c_ref, st_ref, yr_ref, yi_ref, z_ref, x0_ref, bias_ref, o_ref):
    y = _dot(c_ref[...], yr_ref[...]) + _dot(st_ref[...], yi_ref[...])
    o_ref[...] = ((y + z_ref[...] * bias_ref[...]) * x0_ref[...]).astype(BF16)


def _long_conv(cmat, smat, stmat, zb, z, x0, kr, ki, bias, seq_len, tf):
    t, d_h = z.shape
    nb = t // seq_len
    nf = seq_len // tf
    mat_spec = pl.BlockSpec((tf, seq_len), lambda f, b: (f, 0))
    seq_spec = pl.BlockSpec((seq_len, d_h), lambda f, b: (b, 0))
    tile_spec = pl.BlockSpec((tf, d_h), lambda f, b: (b * nf + f, 0))
    k_spec = pl.BlockSpec((tf, d_h), lambda f, b: (f, 0))
    yr, yi = pl.pallas_call(
        functools.partial(_fwd_kernel, n_fft=2 * seq_len),
        grid=(nf, nb),
        in_specs=[mat_spec, mat_spec, seq_spec, k_spec, k_spec],
        out_specs=[tile_spec, tile_spec],
        out_shape=[jax.ShapeDtypeStruct((t, d_h), BF16), jax.ShapeDtypeStruct((t, d_h), BF16)],
        compiler_params=_params(("parallel", "parallel"), 56),
        name="conv_fwd_dft",
    )(cmat, smat, zb, kr, ki)
    return pl.pallas_call(
        _inv_kernel,
        grid=(nf, nb),
        in_specs=[mat_spec, mat_spec, seq_spec, seq_spec, tile_spec, tile_spec,
                  pl.BlockSpec((1, d_h), lambda f, b: (0, 0))],
        out_specs=tile_spec,
        out_shape=jax.ShapeDtypeStruct((t, d_h), BF16),
        compiler_params=_params(("parallel", "parallel"), 56),
        name="conv_inv_dft",
    )(cmat, stmat, yr, yi, z, x0, bias.reshape(1, d_h))


def _top16(s, payload=None):
    n_rows, n = s.shape
    groups = n_rows // SUBLANES
    slabs = [s[g * SUBLANES:(g + 1) * SUBLANES] for g in range(groups)]
    sub = lax.broadcasted_iota(I32, (SUBLANES, n), 0).astype(F32)
    rid = [sub + float(g * SUBLANES) for g in range(groups)]
    vals, ids = [], []
    for _ in range(PEER_TOPK):
        m = jnp.max(functools.reduce(jnp.maximum, slabs), axis=0, keepdims=True)
        first = jnp.full((SUBLANES, n), float(groups), F32)
        for g in reversed(range(groups)):
            first = jnp.where(slabs[g] == m, float(g), first)
        row = jnp.where(first < float(groups), first * float(SUBLANES) + sub, float(n_rows))
        pos = jnp.min(row, axis=0, keepdims=True)
        hits = [r == pos for r in rid]
        vals.append(m)
        if payload is None:
            ids.append(pos)
        else:
            picked = [jnp.where(hits[g], payload[g * SUBLANES:(g + 1) * SUBLANES], -1.0) for g in range(groups)]
            ids.append(jnp.max(functools.reduce(jnp.maximum, picked), axis=0, keepdims=True))
        slabs = [jnp.where(h, -jnp.inf, sl) for h, sl in zip(hits, slabs)]
    return jnp.concatenate(vals, axis=0), jnp.concatenate(ids, axis=0)


def _route_kernel(hyb_ref, p1_ref, sgb_ref, x1_ref, x2_ref, who_ref, wo_ref, g2_ref, wq_ref, keys_ref,
                  h_ref, hn_ref, idx_ref, gate_ref, q_scr, idx_scr, gate_scr, cand_scr, cidx_scr,
                  *, idx_scale, n_first):
    yb = _dot(hyb_ref[...], who_ref[...])
    mixed = p1_ref[...] + sgb_ref[...] * yb
    h = _two_part_block(x1_ref, x2_ref, n_first) + _dot(mixed.astype(BF16), wo_ref[...])
    hn = _rms(h, g2_ref[...])
    for c in range(h.shape[1] // LANES):
        h_ref[:, c, :] = h[:, c * LANES:(c + 1) * LANES]
        hn_ref[:, c, :] = hn[:, c * LANES:(c + 1) * LANES]
    q = _dot(hn.astype(BF16), wq_ref[...])
    n_half = 2 * PEER_HEADS
    dh = q.shape[1] // n_half
    for j in range(n_half):
        q_scr[j] = q[:, j * dh:(j + 1) * dh].astype(BF16)

    def head(hd, carry):
        sv0, si0 = _top16(_dot_nt(keys_ref[2 * hd], q_scr[2 * hd]))
        sv1, si1 = _top16(_dot_nt(keys_ref[2 * hd + 1], q_scr[2 * hd + 1]))
        cand_scr[...] = jnp.full(cand_scr.shape, -jnp.inf, F32)
        cidx_scr[...] = jnp.full(cidx_scr.shape, -1.0, F32)
        off = 0
        for a in range(PEER_TOPK):
            nb = PEER_TOPK // (a + 1)
            cand_scr[off:off + nb, :] = sv0[a:a + 1] + sv1[:nb]
            cidx_scr[off:off + nb, :] = (si0[a:a + 1] * float(PEER_NKEYS) + si1[:nb]) * float(idx_scale)
            off += nb
        best, eid = _top16(cand_scr[...], cidx_scr[...])
        ex = jnp.exp(best - best[0:1])
        r0 = pl.multiple_of(hd * PEER_TOPK, PEER_TOPK)
        idx_scr[pl.ds(r0, PEER_TOPK), :] = eid.astype(I32)
        gate_scr[pl.ds(r0, PEER_TOPK), :] = ex / jnp.sum(ex, axis=0, keepdims=True)
        return carry

    lax.fori_loop(0, PEER_HEADS, head, 0)
    idx_ref[...] = idx_scr[...].T
    gate_ref[...] = gate_scr[...].T


def _route(hyb, p1, sg, x1, x2, who_bf, wo_bf, g2, wq_bf, keys_bf, tm, idx_scale):
    d = x1.shape[1]
    t = x1.shape[0] + x2.shape[0]
    n_first = x1.shape[0] // tm
    d_h = hyb.shape[1]
    n_half, nk, dh = keys_bf.shape
    n_cand = sum(PEER_TOPK // (a + 1) for a in range(PEER_TOPK))
    n_cand = -(-n_cand // SUBLANES) * SUBLANES
    tok = lambda w: pl.BlockSpec((tm, w), lambda i: (i, 0))
    full = lambda a: pl.BlockSpec(a.shape, lambda i: tuple(0 for _ in a.shape))
    tok3 = pl.BlockSpec((tm, d // LANES, LANES), lambda i: (i, 0, 0))
    return pl.pallas_call(
        functools.partial(_route_kernel, idx_scale=idx_scale, n_first=n_first),
        grid=(t // tm,),
        in_specs=[tok(d_h), tok(d), pl.BlockSpec((tm, d), lambda i: (i, 1))] + _two_part_specs(tm, d, n_first) + [
                  full(who_bf), full(wo_bf), pl.BlockSpec((1, d), lambda i: (0, 0)), full(wq_bf), full(keys_bf)],
        out_specs=[tok3, tok3, tok(PEER_SLOTS), tok(PEER_SLOTS)],
        out_shape=[jax.ShapeDtypeStruct((t, d // LANES, LANES), F32), jax.ShapeDtypeStruct((t, d // LANES, LANES), F32),
                   jax.ShapeDtypeStruct((t, PEER_SLOTS), I32), jax.ShapeDtypeStruct((t, PEER_SLOTS), F32)],
        scratch_shapes=[pltpu.VMEM((n_half, tm, dh), BF16),
                        pltpu.VMEM((PEER_SLOTS, tm), I32),
                        pltpu.VMEM((PEER_SLOTS, tm), F32),
                        pltpu.VMEM((n_cand, tm), F32),
                        pltpu.VMEM((n_cand, tm), F32)],
        compiler_params=_params(("parallel",), 56),
        name="route",
    )(hyb, p1, sg, x1, x2, who_bf, wo_bf, g2.reshape(1, d), wq_bf, keys_bf)


def _slot_pair(tab_ref, ids, j, e_a, e_b):
    rows = tab_ref.shape[0] // (PEER_NKEYS * PEER_NKEYS)
    assert 2 * rows == SUBLANES
    lo = tab_ref[pl.ds(pl.multiple_of(ids[j, e_a], rows), rows), :]
    hi = tab_ref[pl.ds(pl.multiple_of(ids[j, e_b], rows), rows), :]
    return jnp.concatenate([lo, hi], axis=0)


def _split2(x):
    hi = x.astype(BF16)
    return hi, (x - hi.astype(F32)).astype(BF16)


def _gelu(x):
    return 0.5 * x * (1.0 + jnp.tanh(math.sqrt(2.0 / math.pi) * (x + 0.044715 * (x * x * x))))


ID_CHUNK = 64
V_SLOTS_PER_DOT = 32


def _id_copy(idx_hbm, chunk, dst, sem):
    return pltpu.make_async_copy(idx_hbm.at[pl.ds(chunk * ID_CHUNK, ID_CHUNK), :], dst, sem)


def _token_pipeline(idx_hbm, id_bufs, sems, tb, region):
    chunks_per_step = tb // ID_CHUNK
    assert chunks_per_step % 2 == 0
    first = pl.program_id(0) * chunks_per_step
    total = pl.num_programs(0) * chunks_per_step

    def two_chunks(k, carry):
        c0 = first + 2 * k

        @pl.when(c0 == 0)
        def _():
            _id_copy(idx_hbm, c0, id_bufs[0], sems.at[0]).start()

        _id_copy(idx_hbm, c0 + 1, id_bufs[1], sems.at[1]).start()
        _id_copy(idx_hbm, c0, id_bufs[0], sems.at[0]).wait()
        region(id_bufs[0], 2 * k * ID_CHUNK)

        @pl.when(c0 + 2 < total)
        def _():
            _id_copy(idx_hbm, c0 + 2, id_bufs[0], sems.at[0]).start()

        _id_copy(idx_hbm, c0 + 1, id_bufs[1], sems.at[1]).wait()
        region(id_bufs[1], (2 * k + 1) * ID_CHUNK)
        return carry

    lax.fori_loop(0, chunks_per_step // 2, two_chunks, 0)


def _id_scratch():
    return [pltpu.SMEM((ID_CHUNK, PEER_SLOTS), I32), pltpu.SMEM((ID_CHUNK, PEER_SLOTS), I32),
            pltpu.SemaphoreType.DMA((2,))]


U_GROUP_ORDER = (0, 2, 1, 3)


def _peer_u_kernel(idx_hbm, x_ref, gate_ref, rep_ref, tab_ref, arep_ref, pre_ref, id_a, id_b, sems):
    tb = x_ref.shape[0]
    rows = tab_ref.shape[0] // (PEER_NKEYS * PEER_NKEYS)
    assert rows == 4, "the sublane butterfly below is written for 4 packed rows per expert"
    ones = jnp.ones((SUBLANES, LANES), BF16)
    sub = lax.broadcasted_iota(I32, (SUBLANES, LANES), 0)
    low_pair = (sub % 4) < 2
    even_row = (sub % 2) == 0

    def compute(t, ids, jt):
        x_lo = jnp.concatenate([x_ref[t, pl.ds(0, rows, stride=2), :]] * 2, axis=0)
        x_hi = jnp.concatenate([x_ref[t, pl.ds(1, rows, stride=2), :]] * 2, axis=0)
        merged = []
        for m in range(PEER_SLOTS // SUBLANES):
            a = []
            for j in range(4):
                e = SUBLANES * m + U_GROUP_ORDER[j]
                w = _slot_pair(tab_ref, ids, jt, e, e + 4)
                p = pltpu.bitcast(w << 16, F32) * x_lo + pltpu.bitcast(w & HI_MASK, F32) * x_hi
                a.append(p + pltpu.roll(p, SUBLANES - 2, axis=0))
            ab = jnp.where(low_pair, a[0], pltpu.roll(a[1], 2, axis=0))
            cd = jnp.where(low_pair, a[2], pltpu.roll(a[3], 2, axis=0))
            ab = ab + pltpu.roll(ab, SUBLANES - 1, axis=0)
            cd = cd + pltpu.roll(cd, SUBLANES - 1, axis=0)
            merged.append(jnp.where(even_row, ab, pltpu.roll(cd, 1, axis=0)))
        p_hi, p_lo = _split2(jnp.concatenate(merged, axis=0))
        sums = _dot_nt(ones, p_hi) + _dot_nt(ones, p_lo)
        pre_ref[pl.ds(t, 1), :] = sums[0:1]

    def region(ids, t_base):
        for j in range(ID_CHUNK):
            compute(t_base + j, ids, j)

    _token_pipeline(idx_hbm, (id_a, id_b), sems, tb, region)
    act = _gelu(pre_ref[...]) * gate_ref[...]
    a1 = act.astype(BF16)
    r1 = act - a1.astype(F32)
    a2, a3 = _split2(r1)
    rep = rep_ref[...]
    arep_ref[...] = _dot(a1, rep) + _dot(a2, rep) + _dot(a3, rep)


def _peer_u(idx, x3, gate, tab, tb):
    t, nch, _ = x3.shape
    rep = np.zeros((PEER_SLOTS, PEER_SLOTS * nch), np.float32)
    for e in range(PEER_SLOTS):
        rep[e, e * nch:(e + 1) * nch] = 1.0
    return pl.pallas_call(
        _peer_u_kernel,
        grid=(t // tb,),
        in_specs=[pl.BlockSpec(memory_space=pl.ANY),
                  pl.BlockSpec((tb, nch, LANES), lambda i: (i, 0, 0)),
                  pl.BlockSpec((tb, PEER_SLOTS), lambda i: (i, 0)),
                  pl.BlockSpec(rep.shape, lambda i: (0, 0)),
                  pl.BlockSpec(tab.shape, lambda i: (0, 0), pipeline_mode=pl.Buffered(1))],
        out_specs=pl.BlockSpec((tb, PEER_SLOTS * nch), lambda i: (i, 0)),
        out_shape=jax.ShapeDtypeStruct((t, PEER_SLOTS * nch), F32),
        scratch_shapes=[pltpu.VMEM((tb, PEER_SLOTS), F32)] + _id_scratch(),
        compiler_params=_params(("arbitrary",), 48),
        name="peer_u",
    )(idx, x3, gate, jnp.asarray(rep, BF16), tab)


def _peer_v_kernel(idx_hbm, arep_ref, h_ref, g_ref, tab_ref, o1_ref, o2_ref, y_ref, id_a, id_b, sems, *, n_first):
    tb, nch, _ = h_ref.shape
    shape = (nch, arep_ref.shape[1])
    own_chunk = (lax.broadcasted_iota(I32, shape, 1) % nch) == lax.broadcasted_iota(I32, shape, 0)

    def compute(t, ids, j):
        w_hi, w_lo = _split2(jnp.where(own_chunk, arep_ref[pl.ds(t, 1), :], 0.0))
        acc = None
        for e0 in range(0, PEER_SLOTS, V_SLOTS_PER_DOT):
            pieces = [_slot_pair(tab_ref, ids, j, e, e + 1) for e in range(e0, e0 + V_SLOTS_PER_DOT, 2)]
            vals = pltpu.bitcast(jnp.concatenate(pieces, axis=0), BF16)
            k0, k1 = e0 * nch, (e0 + V_SLOTS_PER_DOT) * nch
            part = _dot(w_hi[:, k0:k1], vals) + _dot(w_lo[:, k0:k1], vals)
            acc = part if acc is None else acc + part
        y_ref[t] = acc

    def region(ids, t_base):
        for j in range(ID_CHUNK):
            compute(t_base + j, ids, j)

    _token_pipeline(idx_hbm, (id_a, id_b), sems, tb, region)
    y = h_ref[...] + y_ref[...]
    ms = jnp.sum(jnp.sum(y * y, axis=2, keepdims=True), axis=1, keepdims=True) * (1.0 / (y.shape[1] * y.shape[2]))
    y_ref[...] = y * lax.rsqrt(ms + EPS) * g_ref[...]

    def emit(o_ref):
        for c in range(nch):
            o_ref[:, c * LANES:(c + 1) * LANES] = y_ref[:, c, :]

    @pl.when(pl.program_id(0) < n_first)
    def _():
        emit(o1_ref)

    @pl.when(pl.program_id(0) >= n_first)
    def _():
        emit(o2_ref)


def _peer_v(idx, arep, h3, g, tab, tb, t_first):
    t, nch, _ = h3.shape
    d = nch * LANES
    n_first = t_first // tb
    blk = pl.BlockSpec((tb, nch, LANES), lambda i: (i, 0, 0))
    return pl.pallas_call(
        functools.partial(_peer_v_kernel, n_first=n_first),
        grid=(t // tb,),
        in_specs=[pl.BlockSpec(memory_space=pl.ANY),
                  pl.BlockSpec((tb, arep.shape[1]), lambda i: (i, 0)),
                  blk,
                  pl.BlockSpec((1, nch, LANES), lambda i: (0, 0, 0)),
                  pl.BlockSpec(tab.shape, lambda i: (0, 0), pipeline_mode=pl.Buffered(1))],
        out_specs=[pl.BlockSpec((tb, d), lambda i: (jnp.minimum(i, n_first - 1), 0)),
                   pl.BlockSpec((tb, d), lambda i: (jnp.maximum(i - n_first, 0), 0))],
        out_shape=[jax.ShapeDtypeStruct((t_first, d), F32), jax.ShapeDtypeStruct((t - t_first, d), F32)],
        scratch_shapes=[pltpu.VMEM((tb, nch, LANES), F32)] + _id_scratch(),
        compiler_params=_params(("arbitrary",), 48),
        name="peer_v",
    )(idx, arep, h3, g.reshape(1, nch, LANES), tab)


def _pick_tile(n, want):
    while n % want:
        want //= 2
    return want


def kernel(x_prompt, x_sample, norm1_g, w_in, pool_w, pool_scale, conv_w, conv_b, filt_w1, filt_b1, filt_f1, filt_w2, filt_b2, filt_f2, filt_w3, hyena_bias, w_pool_out, w_hyena_out, w_o, norm2_g, peer_wq, peer_keys, peer_u, peer_v, normf_g):
    assert norm1_g.shape[0] == 1, "single-layer block"
    assert x_prompt.shape[1:] == x_sample.shape[1:]
    nb_p, seq_len, d = x_prompt.shape
    d_pool = pool_scale.shape[-1]
    d_hy = conv_b.shape[-1]
    d_h = d_hy // 3
    x1 = x_prompt.reshape(-1, d)
    x2 = x_sample.reshape(-1, d)
    tm = _pick_tile(seq_len, 512)

    ahy, sg = _inproj(x1, x2, norm1_g[0], w_in[0].astype(BF16), d_pool + d_hy, tm)
    p1, x0, z, zb = _local(ahy, sg, pool_w[0].astype(BF16), pool_scale[0], conv_w[0], conv_b[0],
                           w_pool_out[0].astype(BF16), seq_len, d_pool, d_hy, tm)
    ks, kd = _filters(filt_w1[0], filt_b1[0], filt_f1[0], filt_w2[0], filt_b2[0], filt_f2[0], filt_w3[0],
                      seq_len, d_h)
    cmat, smat, stmat = _dftgen(seq_len)
    kr, ki = _spectrum(cmat, smat, ks, kd, tm)
    hyb = _long_conv(cmat, smat, stmat, zb, z, x0, kr, ki, hyena_bias[0], seq_len, tm)

    keys = peer_keys[0]
    keys_bf = keys.reshape((-1,) + keys.shape[2:]).astype(BF16)
    h, hn, idx, gate = _route(hyb, p1, sg, x1, x2, w_hyena_out[0].astype(BF16), w_o[0].astype(BF16), norm2_g[0],
                              peer_wq[0].astype(BF16), keys_bf, _pick_tile(seq_len, 512),
                              idx_scale=d // (2 * LANES))

    tb = _pick_tile(seq_len, 256)
    arep = _peer_u(idx, hn, gate, _pack_table(peer_u[0]), tb)
    y_p, y_s = _peer_v(idx, arep, h, normf_g, _pack_table(peer_v[0]), tb,
                       nb_p * seq_len)
    return (y_p.reshape(x_prompt.shape), y_s.reshape(x_sample.shape))
```

```python
import functools
import math

import jax
import jax.numpy as jnp
import numpy as np
from jax import lax
from jax.experimental import pallas as pl
from jax.experimental.pallas import tpu as pltpu

F32 = jnp.float32
BF16 = jnp.bfloat16
I32 = jnp.int32

EPS = 1e-6
POOL_WINDOWS = (2, 4, 8, 16)
POOL_HALO = 8
PEER_HEADS = 8
PEER_NKEYS = 128
PEER_TOPK = 16
PEER_SLOTS = PEER_HEADS * PEER_TOPK
FILTER_EMB = 33
DECAY_TARGET = 1e-2
FAST_DECAY_PCT = 0.3
SLOW_DECAY_PCT = 1.5
LANES = 128
SUBLANES = 8
HI_MASK = -65536
MIB = 1024 * 1024


def _params(sem, vmem_mib):
    return pltpu.CompilerParams(dimension_semantics=sem, vmem_limit_bytes=vmem_mib * MIB)


def _dot(a, b):
    return jnp.dot(a, b, preferred_element_type=F32)


def _dot_nt(a, b):
    return lax.dot_general(a, b, (((1,), (1,)), ((), ())), preferred_element_type=F32)


def _dot_f32(a, b):
    return jnp.dot(a, b, preferred_element_type=F32, precision=lax.Precision.HIGHEST)


def _rms(x, g):
    return x * lax.rsqrt(jnp.mean(x * x, axis=-1, keepdims=True) + EPS) * g


def _pack_kernel(t_ref, o_ref):
    x = t_ref[...]
    for r in range(x.shape[1] // (2 * LANES)):
        lo = pltpu.bitcast(x[:, (2 * r) * LANES:(2 * r + 1) * LANES].astype(BF16).astype(F32), I32)
        hi = pltpu.bitcast(x[:, (2 * r + 1) * LANES:(2 * r + 2) * LANES].astype(BF16).astype(F32), I32)
        o_ref[:, r * LANES:(r + 1) * LANES] = hi | lax.shift_right_logical(lo, 16)


def _pack_table(tab):
    e, d = tab.shape
    rows = 512
    packed = pl.pallas_call(
        _pack_kernel,
        grid=(e // rows,),
        in_specs=[pl.BlockSpec((rows, d), lambda i: (i, 0))],
        out_specs=pl.BlockSpec((rows, d // 2), lambda i: (i, 0)),
        out_shape=jax.ShapeDtypeStruct((e, d // 2), I32),
        compiler_params=_params(("parallel",), 32),
        name="pack_table",
    )(tab)
    return packed.reshape(e * (d // 2) // LANES, LANES)


def _two_part_specs(tm, d, n_first):
    return [pl.BlockSpec((tm, d), lambda i: (jnp.minimum(i, n_first - 1), 0)),
            pl.BlockSpec((tm, d), lambda i: (jnp.maximum(i - n_first, 0), 0))]


def _two_part_block(x1_ref, x2_ref, n_first):
    return jnp.where(pl.program_id(0) < n_first, x1_ref[...], x2_ref[...])


def _inproj_kernel(x1_ref, x2_ref, g_ref, w_ref, ahy_ref, sg_ref, *, n_ahy, chunk, n_first):
    xn = _rms(_two_part_block(x1_ref, x2_ref, n_first), g_ref[...]).astype(BF16)
    for j in range(n_ahy // chunk):
        ahy_ref[:, j * chunk:(j + 1) * chunk] = _dot(xn, w_ref[:, j * chunk:(j + 1) * chunk])
    n_g = w_ref.shape[1] - n_ahy
    for j in range(n_g // chunk):
        g = _dot(xn, w_ref[:, n_ahy + j * chunk:n_ahy + (j + 1) * chunk])
        sg_ref[:, j * chunk:(j + 1) * chunk] = 1.0 / (1.0 + jnp.exp(-g))


def _inproj(x1, x2, g, w_bf, n_ahy, tm):
    d = x1.shape[1]
    t = x1.shape[0] + x2.shape[0]
    n = w_bf.shape[1]
    n_first = x1.shape[0] // tm
    return pl.pallas_call(
        functools.partial(_inproj_kernel, n_ahy=n_ahy, chunk=min(1024, n_ahy, n - n_ahy), n_first=n_first),
        grid=(t // tm,),
        in_specs=_two_part_specs(tm, d, n_first) + [
                  pl.BlockSpec((1, d), lambda i: (0, 0)),
                  pl.BlockSpec((d, n), lambda i: (0, 0))],
        out_specs=[pl.BlockSpec((tm, n_ahy), lambda i: (i, 0)),
                   pl.BlockSpec((tm, n - n_ahy), lambda i: (i, 0))],
        out_shape=[jax.ShapeDtypeStruct((t, n_ahy), F32), jax.ShapeDtypeStruct((t, n - n_ahy), F32)],
        compiler_params=_params(("parallel",), 56),
        name="inproj",
    )(x1, x2, g.reshape(1, d), w_bf)


def _local_kernel(ahy_ref, prev_ref, next_ref, sga_ref, pw_ref, ps_ref, cw_ref, cb_ref, wpo_ref,
                  p1_ref, x0_ref, z_ref, zb_ref, *, seq_len, d_pool, d_hy):
    tm = ahy_ref.shape[0]
    ext_rows = tm + 2 * POOL_HALO
    tiles_per_seq = seq_len // tm
    j = pl.program_id(0) % tiles_per_seq
    prev = jnp.where(j == 0, 0.0, prev_ref[...])
    nxt = jnp.where(j == tiles_per_seq - 1, 0.0, next_ref[...])
    pos = j * tm + lax.broadcasted_iota(I32, (tm, 1), 0)

    def ext(lo, hi):
        return jnp.concatenate([prev[:, lo:hi], ahy_ref[:, lo:hi], nxt[:, lo:hi]], axis=0)

    def shifted(v, k):
        return pltpu.roll(v, k % ext_rows, axis=0)

    group = d_pool // len(POOL_WINDOWS)
    mixed = []
    for g, w in enumerate(POOL_WINDOWS):
        half = w // 2
        e = ext(g * group, (g + 1) * group)
        win = e + shifted(e, 1)
        span = 1
        while span < half:
            win = shifted(win, span) + shifted(win, -span)
            span *= 2
        win = win[POOL_HALO:POOL_HALO + tm]
        cnt = (jnp.minimum(pos + half, seq_len) - jnp.maximum(pos - half, 0)).astype(F32)
        pooled = win / cnt - ahy_ref[:, g * group:(g + 1) * group]
        mixed.append(_dot(pooled.astype(BF16), pw_ref[g]))
    pm = jnp.concatenate(mixed, axis=1) * ps_ref[...]
    p1_ref[...] = sga_ref[...] * _dot(pm.astype(BF16), wpo_ref[...])

    d_br = d_hy // 3
    branches = []
    for b in range(3):
        lo, hi = d_pool + b * d_br, d_pool + (b + 1) * d_br
        e = ext(lo, hi)
        c0, c1 = b * d_br, (b + 1) * d_br
        uc = (shifted(e, 1)[POOL_HALO:POOL_HALO + tm] * cw_ref[0:1, c0:c1]
              + ahy_ref[:, lo:hi] * cw_ref[1:2, c0:c1]
              + shifted(e, -1)[POOL_HALO:POOL_HALO + tm] * cw_ref[2:3, c0:c1]
              + cb_ref[:, c0:c1])
        branches.append(uc)
    x0, x1, v = branches
    z = v * x1
    x0_ref[...] = x0
    z_ref[...] = z
    zb_ref[...] = z.astype(BF16)


def _local(ahy, sg, pool_w_bf, pool_scale, conv_w, conv_b, w_pool_out_bf, seq_len, d_pool, d_hy, tm):
    t, n_ahy = ahy.shape
    d_model = w_pool_out_bf.shape[1]
    d_br = d_hy // 3
    hb = tm // POOL_HALO
    n_halo_blocks = t // POOL_HALO
    return pl.pallas_call(
        functools.partial(_local_kernel, seq_len=seq_len, d_pool=d_pool, d_hy=d_hy),
        grid=(t // tm,),
        in_specs=[pl.BlockSpec((tm, n_ahy), lambda i: (i, 0)),
                  pl.BlockSpec((POOL_HALO, n_ahy), lambda i: (jnp.maximum(i * hb - 1, 0), 0)),
                  pl.BlockSpec((POOL_HALO, n_ahy), lambda i: (jnp.minimum((i + 1) * hb, n_halo_blocks - 1), 0)),
                  pl.BlockSpec((tm, d_model), lambda i: (i, 0)),
                  pl.BlockSpec(pool_w_bf.shape, lambda i: (0, 0, 0)),
                  pl.BlockSpec((1, d_pool), lambda i: (0, 0)),
                  pl.BlockSpec(conv_w.shape, lambda i: (0, 0)),
                  pl.BlockSpec((1, d_hy), lambda i: (0, 0)),
                  pl.BlockSpec(w_pool_out_bf.shape, lambda i: (0, 0))],
        out_specs=[pl.BlockSpec((tm, d_model), lambda i: (i, 0)),
                   pl.BlockSpec((tm, d_br), lambda i: (i, 0)),
                   pl.BlockSpec((tm, d_br), lambda i: (i, 0)),
                   pl.BlockSpec((tm, d_br), lambda i: (i, 0))],
        out_shape=[jax.ShapeDtypeStruct((t, d_model), F32),
                   jax.ShapeDtypeStruct((t, d_br), F32),
                   jax.ShapeDtypeStruct((t, d_br), F32),
                   jax.ShapeDtypeStruct((t, d_br), BF16)],
        compiler_params=_params(("parallel",), 56),
        name="local_mix",
    )(ahy, ahy, ahy, sg, pool_w_bf, pool_scale.reshape(1, d_pool), conv_w, conv_b.reshape(1, d_hy), w_pool_out_bf)


def _filter_kernel(freq_ref, w1_ref, b1_ref, f1_ref, w2_ref, b2_ref, f2_ref, w3_ref, dl_ref, ks_ref, kd_ref,
                   *, seq_len):
    tl = ks_ref.shape[0]
    d_h = ks_ref.shape[1]
    i = (pl.program_id(0) * tl + lax.broadcasted_iota(I32, (tl, 1), 0)).astype(F32)
    t = i / (seq_len - 1.0)
    wpos = (2.0 * math.pi) * i / seq_len
    lane = lax.broadcasted_iota(I32, (tl, LANES), 1)
    bands = (FILTER_EMB - 1) // 2
    ang = freq_ref[...] * wpos
    feat = jnp.where(lane == 0, t,
                     jnp.where(lane <= bands, jnp.cos(ang),
                               jnp.where(lane <= 2 * bands, -jnp.sin(ang), 0.0)))
    h = jnp.sin(f1_ref[...] * (_dot_f32(feat, w1_ref[...]) + b1_ref[...]))
    h = jnp.sin(f2_ref[...] * (_dot_f32(h, w2_ref[...]) + b2_ref[...]))
    h = _dot_f32(h, w3_ref[...])
    decay = jnp.exp(-t * jnp.abs(dl_ref[...]))
    kf = h[:, :d_h] * decay
    kb = h[:, d_h:] * decay
    ks_ref[...] = kf + kb
    kd_ref[...] = kf - kb


def _filters(w1, b1, f1, w2, b2, f2, w3, seq_len, d_h):
    hid = w1.shape[1]
    bands = (FILTER_EMB - 1) // 2
    freqs = np.zeros((1, LANES), np.float32)
    fr = np.linspace(1e-4, bands - 1, bands, dtype=np.float32)
    freqs[0, 1:1 + bands] = fr
    freqs[0, 1 + bands:1 + 2 * bands] = fr
    w1p = jnp.zeros((LANES, hid), F32).at[:FILTER_EMB].set(w1)
    max_decay = math.log(DECAY_TARGET) / FAST_DECAY_PCT
    min_decay = math.log(DECAY_TARGET) / SLOW_DECAY_PCT
    deltas = np.linspace(min_decay, max_decay, d_h, dtype=np.float32).reshape(1, d_h)
    tl = min(512, seq_len)
    full = lambda shape: pl.BlockSpec(shape, lambda i: tuple(0 for _ in shape))
    return pl.pallas_call(
        functools.partial(_filter_kernel, seq_len=seq_len),
        grid=(seq_len // tl,),
        in_specs=[full((1, LANES)), full((LANES, hid)), full((1, hid)), full((1, hid)),
                  full((hid, hid)), full((1, hid)), full((1, hid)), full((hid, 2 * d_h)), full((1, d_h))],
        out_specs=[pl.BlockSpec((tl, d_h), lambda i: (i, 0)), pl.BlockSpec((tl, d_h), lambda i: (i, 0))],
        out_shape=[jax.ShapeDtypeStruct((seq_len, d_h), F32), jax.ShapeDtypeStruct((seq_len, d_h), F32)],
        compiler_params=_params(("parallel",), 32),
        name="hyena_filters",
    )(jnp.asarray(freqs), w1p, b1.reshape(1, hid), f1.reshape(1, hid), w2, b2.reshape(1, hid),
      f2.reshape(1, hid), w3, jnp.asarray(deltas))


def _dftgen_kernel(c_ref, s_ref, st_ref, cb_ref, sb_ref, *, n_fft):
    tr, n = c_ref.shape
    theta = 2.0 * math.pi / n_fft
    in_tile = lax.broadcasted_iota(I32, (tr, n), 0)
    col = lax.broadcasted_iota(I32, (tr, n), 1)

    @pl.when(pl.program_id(0) == 0)
    def _():
        ang = ((in_tile * col) & (n_fft - 1)).astype(F32) * theta
        cb_ref[...] = jnp.cos(ang)
        sb_ref[...] = jnp.sin(ang)

    row0 = pl.program_id(0) * tr
    ang0 = ((row0 * lax.broadcasted_iota(I32, (1, n), 1)) & (n_fft - 1)).astype(F32) * theta
    ca, sa = jnp.cos(ang0), jnp.sin(ang0)
    cb, sb = cb_ref[...], sb_ref[...]
    row = row0 + in_tile
    c_ref[...] = (ca * cb - sa * sb).astype(BF16)
    ms = -(sa * cb + ca * sb)
    sign_col = (1 - 2 * (col & 1)).astype(F32)
    sign_row = (1 - 2 * (row & 1)).astype(F32)
    s_ref[...] = jnp.where(row == 0, sign_col, ms).astype(BF16)
    st_ref[...] = jnp.where(col == 0, sign_row, ms).astype(BF16)


def _dftgen(seq_len):
    tr = min(256, seq_len)
    spec = pl.BlockSpec((tr, seq_len), lambda i: (i, 0))
    shp = jax.ShapeDtypeStruct((seq_len, seq_len), BF16)
    return pl.pallas_call(
        functools.partial(_dftgen_kernel, n_fft=2 * seq_len),
        grid=(seq_len // tr,),
        out_specs=[spec, spec, spec],
        out_shape=[shp, shp, shp],
        scratch_shapes=[pltpu.VMEM((tr, seq_len), F32), pltpu.VMEM((tr, seq_len), F32)],
        compiler_params=_params(("arbitrary",), 48),
        name="dft_matrices",
    )()


def _spectrum_kernel(c_ref, s_ref, ks_ref, kd_ref, kr_ref, ki_ref):
    ks = ks_ref[...]
    kr_ref[...] = _dot(c_ref[...], ks.astype(BF16))
    ki = _dot(s_ref[...], kd_ref[...].astype(BF16))
    tf = ki.shape[0]
    row = lax.broadcasted_iota(I32, (tf, 1), 0)
    sign = (1 - 2 * (lax.broadcasted_iota(I32, (ks.shape[0], 1), 0) & 1)).astype(F32)
    nyq = jnp.sum(ks * sign, axis=0, keepdims=True)
    first = jnp.logical_and(pl.program_id(0) == 0, row == 0)
    ki_ref[...] = jnp.where(first, nyq, ki)


def _spectrum(cmat, smat, ks, kd, tf):
    n, d_h = ks.shape
    return pl.pallas_call(
        _spectrum_kernel,
        grid=(n // tf,),
        in_specs=[pl.BlockSpec((tf, n), lambda i: (i, 0)), pl.BlockSpec((tf, n), lambda i: (i, 0)),
                  pl.BlockSpec((n, d_h), lambda i: (0, 0)), pl.BlockSpec((n, d_h), lambda i: (0, 0))],
        out_specs=[pl.BlockSpec((tf, d_h), lambda i: (i, 0)), pl.BlockSpec((tf, d_h), lambda i: (i, 0))],
        out_shape=[jax.ShapeDtypeStruct((n, d_h), F32), jax.ShapeDtypeStruct((n, d_h), F32)],
        compiler_params=_params(("parallel",), 56),
        name="filter_spectrum",
    )(cmat, smat, ks, kd)


def _fwd_kernel(c_ref, s_ref, z_ref, kr_ref, ki_ref, yr_ref, yi_ref, *, n_fft):
    z = z_ref[...]
    zr = _dot(c_ref[...], z)
    zi = _dot(s_ref[...], z)
    kr = kr_ref[...]
    ki = ki_ref[...]
    row = lax.broadcasted_iota(I32, (zr.shape[0], 1), 0)
    first = jnp.logical_and(pl.program_id(0) == 0, row == 0)
    yr = jnp.where(first, zr * kr, zr * kr - zi * ki)
    yi = jnp.where(first, zi * ki, zr * ki + zi * kr)
    scale = jnp.where(first, 1.0 / n_fft, 2.0 / n_fft)
    yr_ref[...] = (yr * scale).astype(BF16)
    yi_ref[...] = (yi * scale).astype(BF16)


def _inv_kernel(c_ref, st_ref, yr_ref, yi_ref, z_ref, x0_ref, bias_ref, o_ref):
    y = _dot(c_ref[...], yr_ref[...]) + _dot(st_ref[...], yi_ref[...])
    o_ref[...] = ((y + z_ref[...] * bias_ref[...]) * x0_ref[...]).astype(BF16)


def _long_conv(cmat, smat, stmat, zb, z, x0, kr, ki, bias, seq_len, tf):
    t, d_h = z.shape
    nb = t // seq_len
    nf = seq_len // tf
    mat_spec = pl.BlockSpec((tf, seq_len), lambda f, b: (f, 0))
    seq_spec = pl.BlockSpec((seq_len, d_h), lambda f, b: (b, 0))
    tile_spec = pl.BlockSpec((tf, d_h), lambda f, b: (b * nf + f, 0))
    k_spec = pl.BlockSpec((tf, d_h), lambda f, b: (f, 0))
    yr, yi = pl.pallas_call(
        functools.partial(_fwd_kernel, n_fft=2 * seq_len),
        grid=(nf, nb),
        in_specs=[mat_spec, mat_spec, seq_spec, k_spec, k_spec],
        out_specs=[tile_spec, tile_spec],
        out_shape=[jax.ShapeDtypeStruct((t, d_h), BF16), jax.ShapeDtypeStruct((t, d_h), BF16)],
        compiler_params=_params(("parallel", "parallel"), 56),
        name="conv_fwd_dft",
    )(cmat, smat, zb, kr, ki)
    return pl.pallas_call(
        _inv_kernel,
        grid=(nf, nb),
        in_specs=[mat_spec, mat_spec, seq_spec, seq_spec, tile_spec, tile_spec,
                  pl.BlockSpec((1, d_h), lambda f, b: (0, 0))],
        out_specs=tile_spec,
        out_shape=jax.ShapeDtypeStruct((t, d_h), BF16),
        compiler_params=_params(("parallel", "parallel"), 56),
        name="conv_inv_dft",
    )(cmat, stmat, yr, yi, z, x0, bias.reshape(1, d_h))


def _top16(s, payload=None):
    n_rows, n = s.shape
    groups = n_rows // SUBLANES
    slabs = [s[g * SUBLANES:(g + 1) * SUBLANES] for g in range(groups)]
    sub = lax.broadcasted_iota(I32, (SUBLANES, n), 0).astype(F32)
    rid = [sub + float(g * SUBLANES) for g in range(groups)]
    vals, ids = [], []
    for _ in range(PEER_TOPK):
        m = jnp.max(functools.reduce(jnp.maximum, slabs), axis=0, keepdims=True)
        first = jnp.full((SUBLANES, n), float(groups), F32)
        for g in reversed(range(groups)):
            first = jnp.where(slabs[g] == m, float(g), first)
        row = jnp.where(first < float(groups), first * float(SUBLANES) + sub, float(n_rows))
        pos = jnp.min(row, axis=0, keepdims=True)
        hits = [r == pos for r in rid]
        vals.append(m)
        if payload is None:
            ids.append(pos)
        else:
            picked = [jnp.where(hits[g], payload[g * SUBLANES:(g + 1) * SUBLANES], -1.0) for g in range(groups)]
            ids.append(jnp.max(functools.reduce(jnp.maximum, picked), axis=0, keepdims=True))
        slabs = [jnp.where(h, -jnp.inf, sl) for h, sl in zip(hits, slabs)]
    return jnp.concatenate(vals, axis=0), jnp.concatenate(ids, axis=0)


def _route_kernel(hyb_ref, p1_ref, sgb_ref, x1_ref, x2_ref, who_ref, wo_ref, g2_ref, wq_ref, keys_ref,
                  h_ref, hn_ref, idx_ref, gate_ref, q_scr, idx_scr, gate_scr, cand_scr, cidx_scr,
                  *, idx_scale, n_first):
    yb = _dot(hyb_ref[...], who_ref[...])
    mixed = p1_ref[...] + sgb_ref[...] * yb
    h = _two_part_block(x1_ref, x2_ref, n_first) + _dot(mixed.astype(BF16), wo_ref[...])
    hn = _rms(h, g2_ref[...])
    for c in range(h.shape[1] // LANES):
        h_ref[:, c, :] = h[:, c * LANES:(c + 1) * LANES]
        hn_ref[:, c, :] = hn[:, c * LANES:(c + 1) * LANES]
    q = _dot(hn.astype(BF16), wq_ref[...])
    n_half = 2 * PEER_HEADS
    dh = q.shape[1] // n_half
    for j in range(n_half):
        q_scr[j] = q[:, j * dh:(j + 1) * dh].astype(BF16)

    def head(hd, carry):
        sv0, si0 = _top16(_dot_nt(keys_ref[2 * hd], q_scr[2 * hd]))
        sv1, si1 = _top16(_dot_nt(keys_ref[2 * hd + 1], q_scr[2 * hd + 1]))
        cand_scr[...] = jnp.full(cand_scr.shape, -jnp.inf, F32)
        cidx_scr[...] = jnp.full(cidx_scr.shape, -1.0, F32)
        off = 0
        for a in range(PEER_TOPK):
            nb = PEER_TOPK // (a + 1)
            cand_scr[off:off + nb, :] = sv0[a:a + 1] + sv1[:nb]
            cidx_scr[off:off + nb, :] = (si0[a:a + 1] * float(PEER_NKEYS) + si1[:nb]) * float(idx_scale)
            off += nb
        best, eid = _top16(cand_scr[...], cidx_scr[...])
        ex = jnp.exp(best - best[0:1])
        r0 = pl.multiple_of(hd * PEER_TOPK, PEER_TOPK)
        idx_scr[pl.ds(r0, PEER_TOPK), :] = eid.astype(I32)
        gate_scr[pl.ds(r0, PEER_TOPK), :] = ex / jnp.sum(ex, axis=0, keepdims=True)
        return carry

    lax.fori_loop(0, PEER_HEADS, head, 0)
    idx_ref[...] = idx_scr[...].T
    gate_ref[...] = gate_scr[...].T


def _route(hyb, p1, sg, x1, x2, who_bf, wo_bf, g2, wq_bf, keys_bf, tm, idx_scale):
    d = x1.shape[1]
    t = x1.shape[0] + x2.shape[0]
    n_first = x1.shape[0] // tm
    d_h = hyb.shape[1]
    n_half, nk, dh = keys_bf.shape
    n_cand = sum(PEER_TOPK // (a + 1) for a in range(PEER_TOPK))
    n_cand = -(-n_cand // SUBLANES) * SUBLANES
    tok = lambda w: pl.BlockSpec((tm, w), lambda i: (i, 0))
    full = lambda a: pl.BlockSpec(a.shape, lambda i: tuple(0 for _ in a.shape))
    tok3 = pl.BlockSpec((tm, d // LANES, LANES), lambda i: (i, 0, 0))
    return pl.pallas_call(
        functools.partial(_route_kernel, idx_scale=idx_scale, n_first=n_first),
        grid=(t // tm,),
        in_specs=[tok(d_h), tok(d), pl.BlockSpec((tm, d), lambda i: (i, 1))] + _two_part_specs(tm, d, n_first) + [
                  full(who_bf), full(wo_bf), pl.BlockSpec((1, d), lambda i: (0, 0)), full(wq_bf), full(keys_bf)],
        out_specs=[tok3, tok3, tok(PEER_SLOTS), tok(PEER_SLOTS)],
        out_shape=[jax.ShapeDtypeStruct((t, d // LANES, LANES), F32), jax.ShapeDtypeStruct((t, d // LANES, LANES), F32),
                   jax.ShapeDtypeStruct((t, PEER_SLOTS), I32), jax.ShapeDtypeStruct((t, PEER_SLOTS), F32)],
        scratch_shapes=[pltpu.VMEM((n_half, tm, dh), BF16),
                        pltpu.VMEM((PEER_SLOTS, tm), I32),
                        pltpu.VMEM((PEER_SLOTS, tm), F32),
                        pltpu.VMEM((n_cand, tm), F32),
                        pltpu.VMEM((n_cand, tm), F32)],
        compiler_params=_params(("parallel",), 56),
        name="route",
    )(hyb, p1, sg, x1, x2, who_bf, wo_bf, g2.reshape(1, d), wq_bf, keys_bf)


def _slot_pair(tab_ref, ids, j, e_a, e_b):
    rows = tab_ref.shape[0] // (PEER_NKEYS * PEER_NKEYS)
    assert 2 * rows == SUBLANES
    lo = tab_ref[pl.ds(pl.multiple_of(ids[j, e_a], rows), rows), :]
    hi = tab_ref[pl.ds(pl.multiple_of(ids[j, e_b], rows), rows), :]
    return jnp.concatenate([lo, hi], axis=0)


def _split2(x):
    hi = x.astype(BF16)
    return hi, (x - hi.astype(F32)).astype(BF16)


def _gelu(x):
    return 0.5 * x * (1.0 + jnp.tanh(math.sqrt(2.0 / math.pi) * (x + 0.044715 * (x * x * x))))


ID_CHUNK = 128
V_SLOTS_PER_DOT = 32


def _id_copy(idx_hbm, chunk, dst, sem):
    return pltpu.make_async_copy(idx_hbm.at[pl.ds(chunk * ID_CHUNK, ID_CHUNK), :], dst, sem)


def _token_pipeline(idx_hbm, id_bufs, sems, tb, region):
    chunks_per_step = tb // ID_CHUNK
    assert chunks_per_step % 2 == 0
    first = pl.program_id(0) * chunks_per_step
    total = pl.num_programs(0) * chunks_per_step

    def two_chunks(k, carry):
        c0 = first + 2 * k

        @pl.when(c0 == 0)
        def _():
            _id_copy(idx_hbm, c0, id_bufs[0], sems.at[0]).start()

        _id_copy(idx_hbm, c0 + 1, id_bufs[1], sems.at[1]).start()
        _id_copy(idx_hbm, c0, id_bufs[0], sems.at[0]).wait()
        region(id_bufs[0], 2 * k * ID_CHUNK)

        @pl.when(c0 + 2 < total)
        def _():
            _id_copy(idx_hbm, c0 + 2, id_bufs[0], sems.at[0]).start()

        _id_copy(idx_hbm, c0 + 1, id_bufs[1], sems.at[1]).wait()
        region(id_bufs[1], (2 * k + 1) * ID_CHUNK)
        return carry

    lax.fori_loop(0, chunks_per_step // 2, two_chunks, 0)


def _id_scratch():
    return [pltpu.SMEM((ID_CHUNK, PEER_SLOTS), I32), pltpu.SMEM((ID_CHUNK, PEER_SLOTS), I32),
            pltpu.SemaphoreType.DMA((2,))]


U_GROUP_ORDER = (0, 2, 1, 3)


def _peer_u_kernel(idx_hbm, x_ref, gate_ref, rep_ref, tab_ref, arep_ref, pre_ref, id_a, id_b, sems):
    tb = x_ref.shape[0]
    rows = tab_ref.shape[0] // (PEER_NKEYS * PEER_NKEYS)
    assert rows == 4, "the sublane butterfly below is written for 4 packed rows per expert"
    ones = jnp.ones((SUBLANES, LANES), BF16)
    sub = lax.broadcasted_iota(I32, (SUBLANES, LANES), 0)
    low_pair = (sub % 4) < 2
    even_row = (sub % 2) == 0

    def compute(t, ids, jt):
        x_lo = jnp.concatenate([x_ref[t, pl.ds(0, rows, stride=2), :]] * 2, axis=0)
        x_hi = jnp.concatenate([x_ref[t, pl.ds(1, rows, stride=2), :]] * 2, axis=0)
        merged = []
        for m in range(PEER_SLOTS // SUBLANES):
            a = []
            for j in range(4):
                e = SUBLANES * m + U_GROUP_ORDER[j]
                w = _slot_pair(tab_ref, ids, jt, e, e + 4)
                p = pltpu.bitcast(w << 16, F32) * x_lo + pltpu.bitcast(w & HI_MASK, F32) * x_hi
                a.append(p + pltpu.roll(p, SUBLANES - 2, axis=0))
            ab = jnp.where(low_pair, a[0], pltpu.roll(a[1], 2, axis=0))
            cd = jnp.where(low_pair, a[2], pltpu.roll(a[3], 2, axis=0))
            ab = ab + pltpu.roll(ab, SUBLANES - 1, axis=0)
            cd = cd + pltpu.roll(cd, SUBLANES - 1, axis=0)
            merged.append(jnp.where(even_row, ab, pltpu.roll(cd, 1, axis=0)))
        p_hi, p_lo = _split2(jnp.concatenate(merged, axis=0))
        sums = _dot_nt(ones, p_hi) + _dot_nt(ones, p_lo)
        pre_ref[pl.ds(t, 1), :] = sums[0:1]

    def region(ids, t_base):
        for j in range(ID_CHUNK):
            compute(t_base + j, ids, j)

    _token_pipeline(idx_hbm, (id_a, id_b), sems, tb, region)
    act = _gelu(pre_ref[...]) * gate_ref[...]
    a1 = act.astype(BF16)
    r1 = act - a1.astype(F32)
    a2, a3 = _split2(r1)
    rep = rep_ref[...]
    arep_ref[...] = _dot(a1, rep) + _dot(a2, rep) + _dot(a3, rep)


def _peer_u(idx, x3, gate, tab, tb):
    t, nch, _ = x3.shape
    rep = np.zeros((PEER_SLOTS, PEER_SLOTS * nch), np.float32)
    for e in range(PEER_SLOTS):
        rep[e, e * nch:(e + 1) * nch] = 1.0
    return pl.pallas_call(
        _peer_u_kernel,
        grid=(t // tb,),
        in_specs=[pl.BlockSpec(memory_space=pl.ANY),
                  pl.BlockSpec((tb, nch, LANES), lambda i: (i, 0, 0)),
                  pl.BlockSpec((tb, PEER_SLOTS), lambda i: (i, 0)),
                  pl.BlockSpec(rep.shape, lambda i: (0, 0)),
                  pl.BlockSpec(tab.shape, lambda i: (0, 0), pipeline_mode=pl.Buffered(1))],
        out_specs=pl.BlockSpec((tb, PEER_SLOTS * nch), lambda i: (i, 0)),
        out_shape=jax.ShapeDtypeStruct((t, PEER_SLOTS * nch), F32),
        scratch_shapes=[pltpu.VMEM((tb, PEER_SLOTS), F32)] + _id_scratch(),
        compiler_params=_params(("arbitrary",), 48),
        name="peer_u",
    )(idx, x3, gate, jnp.asarray(rep, BF16), tab)


def _peer_v_kernel(idx_hbm, arep_ref, h_ref, g_ref, tab_ref, o1_ref, o2_ref, y_ref, id_a, id_b, sems, *, n_first):
    tb, nch, _ = h_ref.shape
    shape = (nch, arep_ref.shape[1])
    own_chunk = (lax.broadcasted_iota(I32, shape, 1) % nch) == lax.broadcasted_iota(I32, shape, 0)

    def compute(t, ids, j):
        w_hi, w_lo = _split2(jnp.where(own_chunk, arep_ref[pl.ds(t, 1), :], 0.0))
        acc = None
        for e0 in range(0, PEER_SLOTS, V_SLOTS_PER_DOT):
            pieces = [_slot_pair(tab_ref, ids, j, e, e + 1) for e in range(e0, e0 + V_SLOTS_PER_DOT, 2)]
            vals = pltpu.bitcast(jnp.concatenate(pieces, axis=0), BF16)
            k0, k1 = e0 * nch, (e0 + V_SLOTS_PER_DOT) * nch
            part = _dot(w_hi[:, k0:k1], vals) + _dot(w_lo[:, k0:k1], vals)
            acc = part if acc is None else acc + part
        y_ref[t] = acc

    def region(ids, t_base):
        for j in range(ID_CHUNK):
            compute(t_base + j, ids, j)

    _token_pipeline(idx_hbm, (id_a, id_b), sems, tb, region)
    y = h_ref[...] + y_ref[...]
    ms = jnp.sum(jnp.sum(y * y, axis=2, keepdims=True), axis=1, keepdims=True) * (1.0 / (y.shape[1] * y.shape[2]))
    y_ref[...] = y * lax.rsqrt(ms + EPS) * g_ref[...]

    def emit(o_ref):
        for c in range(nch):
            o_ref[:, c * LANES:(c + 1) * LANES] = y_ref[:, c, :]

    @pl.when(pl.program_id(0) < n_first)
    def _():
        emit(o1_ref)

    @pl.when(pl.program_id(0) >= n_first)
    def _():
        emit(o2_ref)


def _peer_v(idx, arep, h3, g, tab, tb, t_first):
    t, nch, _ = h3.shape
    d = nch * LANES
    n_first = t_first // tb
    blk = pl.BlockSpec((tb, nch, LANES), lambda i: (i, 0, 0))
    return pl.pallas_call(
        functools.partial(_peer_v_kernel, n_first=n_first),
        grid=(t // tb,),
        in_specs=[pl.BlockSpec(memory_space=pl.ANY),
                  pl.BlockSpec((tb, arep.shape[1]), lambda i: (i, 0)),
                  blk,
                  pl.BlockSpec((1, nch, LANES), lambda i: (0, 0, 0)),
                  pl.BlockSpec(tab.shape, lambda i: (0, 0), pipeline_mode=pl.Buffered(1))],
        out_specs=[pl.BlockSpec((tb, d), lambda i: (jnp.minimum(i, n_first - 1), 0)),
                   pl.BlockSpec((tb, d), lambda i: (jnp.maximum(i - n_first, 0), 0))],
        out_shape=[jax.ShapeDtypeStruct((t_first, d), F32), jax.ShapeDtypeStruct((t - t_first, d), F32)],
        scratch_shapes=[pltpu.VMEM((tb, nch, LANES), F32)] + _id_scratch(),
        compiler_params=_params(("arbitrary",), 48),
        name="peer_v",
    )(idx, arep, h3, g.reshape(1, nch, LANES), tab)


def _pick_tile(n, want):
    while n % want:
        want //= 2
    return want


def kernel(x_prompt, x_sample, norm1_g, w_in, pool_w, pool_scale, conv_w, conv_b, filt_w1, filt_b1, filt_f1, filt_w2, filt_b2, filt_f2, filt_w3, hyena_bias, w_pool_out, w_hyena_out, w_o, norm2_g, peer_wq, peer_keys, peer_u, peer_v, normf_g):
    assert norm1_g.shape[0] == 1, "single-layer block"
    assert x_prompt.shape[1:] == x_sample.shape[1:]
    nb_p, seq_len, d = x_prompt.shape
    d_pool = pool_scale.shape[-1]
    d_hy = conv_b.shape[-1]
    d_h = d_hy // 3
    x1 = x_prompt.reshape(-1, d)
    x2 = x_sample.reshape(-1, d)
    tm = _pick_tile(seq_len, 512)

    ahy, sg = _inproj(x1, x2, norm1_g[0], w_in[0].astype(BF16), d_pool + d_hy, tm)
    p1, x0, z, zb = _local(ahy, sg, pool_w[0].astype(BF16), pool_scale[0], conv_w[0], conv_b[0],
                           w_pool_out[0].astype(BF16), seq_len, d_pool, d_hy, tm)
    ks, kd = _filters(filt_w1[0], filt_b1[0], filt_f1[0], filt_w2[0], filt_b2[0], filt_f2[0], filt_w3[0],
                      seq_len, d_h)
    cmat, smat, stmat = _dftgen(seq_len)
    kr, ki = _spectrum(cmat, smat, ks, kd, tm)
    hyb = _long_conv(cmat, smat, stmat, zb, z, x0, kr, ki, hyena_bias[0], seq_len, tm)

    keys = peer_keys[0]
    keys_bf = keys.reshape((-1,) + keys.shape[2:]).astype(BF16)
    h, hn, idx, gate = _route(hyb, p1, sg, x1, x2, w_hyena_out[0].astype(BF16), w_o[0].astype(BF16), norm2_g[0],
                              peer_wq[0].astype(BF16), keys_bf, _pick_tile(seq_len, 512),
                              idx_scale=d // (2 * LANES))

    tb = _pick_tile(seq_len, 256)
    arep = _peer_u(idx, hn, gate, _pack_table(peer_u[0]), tb)
    y_p, y_s = _peer_v(idx, arep, h, normf_g, _pack_table(peer_v[0]), tb,
                       nb_p * seq_len)
    return (y_p.reshape(x_prompt.shape), y_s.reshape(x_sample.shape))
```

```python
import functools
import math

import jax
import jax.numpy as jnp
import numpy as np
from jax import lax
from jax.experimental import pallas as pl
from jax.experimental.pallas import tpu as pltpu

F32 = jnp.float32
BF16 = jnp.bfloat16
I32 = jnp.int32

EPS = 1e-6
POOL_WINDOWS = (2, 4, 8, 16)
POOL_HALO = 8
PEER_HEADS = 8
PEER_NKEYS = 128
PEER_TOPK = 16
PEER_SLOTS = PEER_HEADS * PEER_TOPK
FILTER_EMB = 33
DECAY_TARGET = 1e-2
FAST_DECAY_PCT = 0.3
SLOW_DECAY_PCT = 1.5
LANES = 128
SUBLANES = 8
HI_MASK = -65536
MIB = 1024 * 1024


def _params(sem, vmem_mib):
    return pltpu.CompilerParams(dimension_semantics=sem, vmem_limit_bytes=vmem_mib * MIB)


def _dot(a, b):
    return jnp.dot(a, b, preferred_element_type=F32)


def _dot_nt(a, b):
    return lax.dot_general(a, b, (((1,), (1,)), ((), ())), preferred_element_type=F32)


def _dot_f32(a, b):
    return jnp.dot(a, b, preferred_element_type=F32, precision=lax.Precision.HIGHEST)


def _rms(x, g):
    return x * lax.rsqrt(jnp.mean(x * x, axis=-1, keepdims=True) + EPS) * g


def _pack_kernel(t_ref, o_ref):
    x = t_ref[...]
    for r in range(x.shape[1] // (2 * LANES)):
        lo = pltpu.bitcast(x[:, (2 * r) * LANES:(2 * r + 1) * LANES].astype(BF16).astype(F32), I32)
        hi = pltpu.bitcast(x[:, (2 * r + 1) * LANES:(2 * r + 2) * LANES].astype(BF16).astype(F32), I32)
        o_ref[:, r * LANES:(r + 1) * LANES] = hi | lax.shift_right_logical(lo, 16)


def _pack_table(tab):
    e, d = tab.shape
    rows = 512
    packed = pl.pallas_call(
        _pack_kernel,
        grid=(e // rows,),
        in_specs=[pl.BlockSpec((rows, d), lambda i: (i, 0))],
        out_specs=pl.BlockSpec((rows, d // 2), lambda i: (i, 0)),
        out_shape=jax.ShapeDtypeStruct((e, d // 2), I32),
        compiler_params=_params(("parallel",), 32),
        name="pack_table",
    )(tab)
    return jnp.pad(packed.reshape(e * (d // 2) // LANES, LANES), ((SUBLANES // 2, SUBLANES // 2), (0, 0)))


def _two_part_specs(tm, d, n_first):
    return [pl.BlockSpec((tm, d), lambda i: (jnp.minimum(i, n_first - 1), 0)),
            pl.BlockSpec((tm, d), lambda i: (jnp.maximum(i - n_first, 0), 0))]


def _two_part_block(x1_ref, x2_ref, n_first):
    return jnp.where(pl.program_id(0) < n_first, x1_ref[...], x2_ref[...])


def _inproj_kernel(x1_ref, x2_ref, g_ref, w_ref, ahy_ref, sg_ref, *, n_ahy, chunk, n_first):
    xn = _rms(_two_part_block(x1_ref, x2_ref, n_first), g_ref[...]).astype(BF16)
    for j in range(n_ahy // chunk):
        ahy_ref[:, j * chunk:(j + 1) * chunk] = _dot(xn, w_ref[:, j * chunk:(j + 1) * chunk])
    n_g = w_ref.shape[1] - n_ahy
    for j in range(n_g // chunk):
        g = _dot(xn, w_ref[:, n_ahy + j * chunk:n_ahy + (j + 1) * chunk])
        sg_ref[:, j * chunk:(j + 1) * chunk] = 1.0 / (1.0 + jnp.exp(-g))


def _inproj(x1, x2, g, w_bf, n_ahy, tm):
    d = x1.shape[1]
    t = x1.shape[0] + x2.shape[0]
    n = w_bf.shape[1]
    n_first = x1.shape[0] // tm
    return pl.pallas_call(
        functools.partial(_inproj_kernel, n_ahy=n_ahy, chunk=min(1024, n_ahy, n - n_ahy), n_first=n_first),
        grid=(t // tm,),
        in_specs=_two_part_specs(tm, d, n_first) + [
                  pl.BlockSpec((1, d), lambda i: (0, 0)),
                  pl.BlockSpec((d, n), lambda i: (0, 0))],
        out_specs=[pl.BlockSpec((tm, n_ahy), lambda i: (i, 0)),
                   pl.BlockSpec((tm, n - n_ahy), lambda i: (i, 0))],
        out_shape=[jax.ShapeDtypeStruct((t, n_ahy), F32), jax.ShapeDtypeStruct((t, n - n_ahy), F32)],
        compiler_params=_params(("parallel",), 56),
        name="inproj",
    )(x1, x2, g.reshape(1, d), w_bf)


def _local_kernel(ahy_ref, prev_ref, next_ref, sga_ref, pw_ref, ps_ref, cw_ref, cb_ref, wpo_ref,
                  p1_ref, x0_ref, z_ref, zb_ref, *, seq_len, d_pool, d_hy):
    tm = ahy_ref.shape[0]
    ext_rows = tm + 2 * POOL_HALO
    tiles_per_seq = seq_len // tm
    j = pl.program_id(0) % tiles_per_seq
    prev = jnp.where(j == 0, 0.0, prev_ref[...])
    nxt = jnp.where(j == tiles_per_seq - 1, 0.0, next_ref[...])
    pos = j * tm + lax.broadcasted_iota(I32, (tm, 1), 0)

    def ext(lo, hi):
        return jnp.concatenate([prev[:, lo:hi], ahy_ref[:, lo:hi], nxt[:, lo:hi]], axis=0)

    def shifted(v, k):
        return pltpu.roll(v, k % ext_rows, axis=0)

    group = d_pool // len(POOL_WINDOWS)
    mixed = []
    for g, w in enumerate(POOL_WINDOWS):
        half = w // 2
        e = ext(g * group, (g + 1) * group)
        win = e + shifted(e, 1)
        span = 1
        while span < half:
            win = shifted(win, span) + shifted(win, -span)
            span *= 2
        win = win[POOL_HALO:POOL_HALO + tm]
        cnt = (jnp.minimum(pos + half, seq_len) - jnp.maximum(pos - half, 0)).astype(F32)
        pooled = win / cnt - ahy_ref[:, g * group:(g + 1) * group]
        mixed.append(_dot(pooled.astype(BF16), pw_ref[g]))
    pm = jnp.concatenate(mixed, axis=1) * ps_ref[...]
    p1_ref[...] = sga_ref[...] * _dot(pm.astype(BF16), wpo_ref[...])

    d_br = d_hy // 3
    branches = []
    for b in range(3):
        lo, hi = d_pool + b * d_br, d_pool + (b + 1) * d_br
        e = ext(lo, hi)
        c0, c1 = b * d_br, (b + 1) * d_br
        uc = (shifted(e, 1)[POOL_HALO:POOL_HALO + tm] * cw_ref[0:1, c0:c1]
              + ahy_ref[:, lo:hi] * cw_ref[1:2, c0:c1]
              + shifted(e, -1)[POOL_HALO:POOL_HALO + tm] * cw_ref[2:3, c0:c1]
              + cb_ref[:, c0:c1])
        branches.append(uc)
    x0, x1, v = branches
    z = v * x1
    x0_ref[...] = x0
    z_ref[...] = z
    zb_ref[...] = z.astype(BF16)


def _local(ahy, sg, pool_w_bf, pool_scale, conv_w, conv_b, w_pool_out_bf, seq_len, d_pool, d_hy, tm):
    t, n_ahy = ahy.shape
    d_model = w_pool_out_bf.shape[1]
    d_br = d_hy // 3
    hb = tm // POOL_HALO
    n_halo_blocks = t // POOL_HALO
    return pl.pallas_call(
        functools.partial(_local_kernel, seq_len=seq_len, d_pool=d_pool, d_hy=d_hy),
        grid=(t // tm,),
        in_specs=[pl.BlockSpec((tm, n_ahy), lambda i: (i, 0)),
                  pl.BlockSpec((POOL_HALO, n_ahy), lambda i: (jnp.maximum(i * hb - 1, 0), 0)),
                  pl.BlockSpec((POOL_HALO, n_ahy), lambda i: (jnp.minimum((i + 1) * hb, n_halo_blocks - 1), 0)),
                  pl.BlockSpec((tm, d_model), lambda i: (i, 0)),
                  pl.BlockSpec(pool_w_bf.shape, lambda i: (0, 0, 0)),
                  pl.BlockSpec((1, d_pool), lambda i: (0, 0)),
                  pl.BlockSpec(conv_w.shape, lambda i: (0, 0)),
                  pl.BlockSpec((1, d_hy), lambda i: (0, 0)),
                  pl.BlockSpec(w_pool_out_bf.shape, lambda i: (0, 0))],
        out_specs=[pl.BlockSpec((tm, d_model), lambda i: (i, 0)),
                   pl.BlockSpec((tm, d_br), lambda i: (i, 0)),
                   pl.BlockSpec((tm, d_br), lambda i: (i, 0)),
                   pl.BlockSpec((tm, d_br), lambda i: (i, 0))],
        out_shape=[jax.ShapeDtypeStruct((t, d_model), F32),
                   jax.ShapeDtypeStruct((t, d_br), F32),
                   jax.ShapeDtypeStruct((t, d_br), F32),
                   jax.ShapeDtypeStruct((t, d_br), BF16)],
        compiler_params=_params(("parallel",), 56),
        name="local_mix",
    )(ahy, ahy, ahy, sg, pool_w_bf, pool_scale.reshape(1, d_pool), conv_w, conv_b.reshape(1, d_hy), w_pool_out_bf)


def _filter_kernel(freq_ref, w1_ref, b1_ref, f1_ref, w2_ref, b2_ref, f2_ref, w3_ref, dl_ref, ks_ref, kd_ref,
                   *, seq_len):
    tl = ks_ref.shape[0]
    d_h = ks_ref.shape[1]
    i = (pl.program_id(0) * tl + lax.broadcasted_iota(I32, (tl, 1), 0)).astype(F32)
    t = i / (seq_len - 1.0)
    wpos = (2.0 * math.pi) * i / seq_len
    lane = lax.broadcasted_iota(I32, (tl, LANES), 1)
    bands = (FILTER_EMB - 1) // 2
    ang = freq_ref[...] * wpos
    feat = jnp.where(lane == 0, t,
                     jnp.where(lane <= bands, jnp.cos(ang),
                               jnp.where(lane <= 2 * bands, -jnp.sin(ang), 0.0)))
    h = jnp.sin(f1_ref[...] * (_dot_f32(feat, w1_ref[...]) + b1_ref[...]))
    h = jnp.sin(f2_ref[...] * (_dot_f32(h, w2_ref[...]) + b2_ref[...]))
    h = _dot_f32(h, w3_ref[...])
    decay = jnp.exp(-t * jnp.abs(dl_ref[...]))
    kf = h[:, :d_h] * decay
    kb = h[:, d_h:] * decay
    ks_ref[...] = kf + kb
    kd_ref[...] = kf - kb


def _filters(w1, b1, f1, w2, b2, f2, w3, seq_len, d_h):
    hid = w1.shape[1]
    bands = (FILTER_EMB - 1) // 2
    freqs = np.zeros((1, LANES), np.float32)
    fr = np.linspace(1e-4, bands - 1, bands, dtype=np.float32)
    freqs[0, 1:1 + bands] = fr
    freqs[0, 1 + bands:1 + 2 * bands] = fr
    w1p = jnp.zeros((LANES, hid), F32).at[:FILTER_EMB].set(w1)
    max_decay = math.log(DECAY_TARGET) / FAST_DECAY_PCT
    min_decay = math.log(DECAY_TARGET) / SLOW_DECAY_PCT
    deltas = np.linspace(min_decay, max_decay, d_h, dtype=np.float32).reshape(1, d_h)
    tl = min(512, seq_len)
    full = lambda shape: pl.BlockSpec(shape, lambda i: tuple(0 for _ in shape))
    return pl.pallas_call(
        functools.partial(_filter_kernel, seq_len=seq_len),
        grid=(seq_len // tl,),
        in_specs=[full((1, LANES)), full((LANES, hid)), full((1, hid)), full((1, hid)),
                  full((hid, hid)), full((1, hid)), full((1, hid)), full((hid, 2 * d_h)), full((1, d_h))],
        out_specs=[pl.BlockSpec((tl, d_h), lambda i: (i, 0)), pl.BlockSpec((tl, d_h), lambda i: (i, 0))],
        out_shape=[jax.ShapeDtypeStruct((seq_len, d_h), F32), jax.ShapeDtypeStruct((seq_len, d_h), F32)],
        compiler_params=_params(("parallel",), 32),
        name="hyena_filters",
    )(jnp.asarray(freqs), w1p, b1.reshape(1, hid), f1.reshape(1, hid), w2, b2.reshape(1, hid),
      f2.reshape(1, hid), w3, jnp.asarray(deltas))


def _dftgen_kernel(c_ref, s_ref, st_ref, cb_ref, sb_ref, *, n_fft):
    tr, n = c_ref.shape
    theta = 2.0 * math.pi / n_fft
    in_tile = lax.broadcasted_iota(I32, (tr, n), 0)
    col = lax.broadcasted_iota(I32, (tr, n), 1)

    @pl.when(pl.program_id(0) == 0)
    def _():
        ang = ((in_tile * col) & (n_fft - 1)).astype(F32) * theta
        cb_ref[...] = jnp.cos(ang)
        sb_ref[...] = jnp.sin(ang)

    row0 = pl.program_id(0) * tr
    ang0 = ((row0 * lax.broadcasted_iota(I32, (1, n), 1)) & (n_fft - 1)).astype(F32) * theta
    ca, sa = jnp.cos(ang0), jnp.sin(ang0)
    cb, sb = cb_ref[...], sb_ref[...]
    row = row0 + in_tile
    c_ref[...] = (ca * cb - sa * sb).astype(BF16)
    ms = -(sa * cb + ca * sb)
    sign_col = (1 - 2 * (col & 1)).astype(F32)
    sign_row = (1 - 2 * (row & 1)).astype(F32)
    s_ref[...] = jnp.where(row == 0, sign_col, ms).astype(BF16)
    st_ref[...] = jnp.where(col == 0, sign_row, ms).astype(BF16)


def _dftgen(seq_len):
    tr = min(256, seq_len)
    spec = pl.BlockSpec((tr, seq_len), lambda i: (i, 0))
    shp = jax.ShapeDtypeStruct((seq_len, seq_len), BF16)
    return pl.pallas_call(
        functools.partial(_dftgen_kernel, n_fft=2 * seq_len),
        grid=(seq_len // tr,),
        out_specs=[spec, spec, spec],
        out_shape=[shp, shp, shp],
        scratch_shapes=[pltpu.VMEM((tr, seq_len), F32), pltpu.VMEM((tr, seq_len), F32)],
        compiler_params=_params(("arbitrary",), 48),
        name="dft_matrices",
    )()


def _spectrum_kernel(c_ref, s_ref, ks_ref, kd_ref, kr_ref, ki_ref):
    ks = ks_ref[...]
    kr_ref[...] = _dot(c_ref[...], ks.astype(BF16))
    ki = _dot(s_ref[...], kd_ref[...].astype(BF16))
    tf = ki.shape[0]
    row = lax.broadcasted_iota(I32, (tf, 1), 0)
    sign = (1 - 2 * (lax.broadcasted_iota(I32, (ks.shape[0], 1), 0) & 1)).astype(F32)
    nyq = jnp.sum(ks * sign, axis=0, keepdims=True)
    first = jnp.logical_and(pl.program_id(0) == 0, row == 0)
    ki_ref[...] = jnp.where(first, nyq, ki)


def _spectrum(cmat, smat, ks, kd, tf):
    n, d_h = ks.shape
    return pl.pallas_call(
        _spectrum_kernel,
        grid=(n // tf,),
        in_specs=[pl.BlockSpec((tf, n), lambda i: (i, 0)), pl.BlockSpec((tf, n), lambda i: (i, 0)),
                  pl.BlockSpec((n, d_h), lambda i: (0, 0)), pl.BlockSpec((n, d_h), lambda i: (0, 0))],
        out_specs=[pl.BlockSpec((tf, d_h), lambda i: (i, 0)), pl.BlockSpec((tf, d_h), lambda i: (i, 0))],
        out_shape=[jax.ShapeDtypeStruct((n, d_h), F32), jax.ShapeDtypeStruct((n, d_h), F32)],
        compiler_params=_params(("parallel",), 56),
        name="filter_spectrum",
    )(cmat, smat, ks, kd)


def _fwd_kernel(c_ref, s_ref, z_ref, kr_ref, ki_ref, yr_ref, yi_ref, *, n_fft):
    z = z_ref[...]
    zr = _dot(c_ref[...], z)
    zi = _dot(s_ref[...], z)
    kr = kr_ref[...]
    ki = ki_ref[...]
    row = lax.broadcasted_iota(I32, (zr.shape[0], 1), 0)
    first = jnp.logical_and(pl.program_id(0) == 0, row == 0)
    yr = jnp.where(first, zr * kr, zr * kr - zi * ki)
    yi = jnp.where(first, zi * ki, zr * ki + zi * kr)
    scale = jnp.where(first, 1.0 / n_fft, 2.0 / n_fft)
    yr_ref[...] = (yr * scale).astype(BF16)
    yi_ref[...] = (yi * scale).astype(BF16)


def _inv_kernel(c_ref, st_ref, yr_ref, yi_ref, z_ref, x0_ref, bias_ref, o_ref):
    y = _dot(c_ref[...], yr_ref[...]) + _dot(st_ref[...], yi_ref[...])
    o_ref[...] = ((y + z_ref[...] * bias_ref[...]) * x0_ref[...]).astype(BF16)


def _long_conv(cmat, smat, stmat, zb, z, x0, kr, ki, bias, seq_len, tf):
    t, d_h = z.shape
    nb = t // seq_len
    nf = seq_len // tf
    mat_spec = pl.BlockSpec((tf, seq_len), lambda f, b: (f, 0))
    seq_spec = pl.BlockSpec((seq_len, d_h), lambda f, b: (b, 0))
    tile_spec = pl.BlockSpec((tf, d_h), lambda f, b: (b * nf + f, 0))
    k_spec = pl.BlockSpec((tf, d_h), lambda f, b: (f, 0))
    yr, yi = pl.pallas_call(
        functools.partial(_fwd_kernel, n_fft=2 * seq_len),
        grid=(nf, nb),
        in_specs=[mat_spec, mat_spec, seq_spec, k_spec, k_spec],
        out_specs=[tile_spec, tile_spec],
        out_shape=[jax.ShapeDtypeStruct((t, d_h), BF16), jax.ShapeDtypeStruct((t, d_h), BF16)],
        compiler_params=_params(("parallel", "parallel"), 56),
        name="conv_fwd_dft",
    )(cmat, smat, zb, kr, ki)
    return pl.pallas_call(
        _inv_kernel,
        grid=(nf, nb),
        in_specs=[mat_spec, mat_spec, seq_spec, seq_spec, tile_spec, tile_spec,
                  pl.BlockSpec((1, d_h), lambda f, b: (0, 0))],
        out_specs=tile_spec,
        out_shape=jax.ShapeDtypeStruct((t, d_h), BF16),
        compiler_params=_params(("parallel", "parallel"), 56),
        name="conv_inv_dft",
    )(cmat, stmat, yr, yi, z, x0, bias.reshape(1, d_h))


def _top16(s, payload=None):
    n_rows, n = s.shape
    groups = n_rows // SUBLANES
    slabs = [s[g * SUBLANES:(g + 1) * SUBLANES] for g in range(groups)]
    sub = lax.broadcasted_iota(I32, (SUBLANES, n), 0).astype(F32)
    rid = [sub + float(g * SUBLANES) for g in range(groups)]
    vals, ids = [], []
    for _ in range(PEER_TOPK):
        m = jnp.max(functools.reduce(jnp.maximum, slabs), axis=0, keepdims=True)
        first = jnp.full((SUBLANES, n), float(groups), F32)
        for g in reversed(range(groups)):
            first = jnp.where(slabs[g] == m, float(g), first)
        row = jnp.where(first < float(groups), first * float(SUBLANES) + sub, float(n_rows))
        pos = jnp.min(row, axis=0, keepdims=True)
        hits = [r == pos for r in rid]
        vals.append(m)
        if payload is None:
            ids.append(pos)
        else:
            picked = [jnp.where(hits[g], payload[g * SUBLANES:(g + 1) * SUBLANES], -1.0) for g in range(groups)]
            ids.append(jnp.max(functools.reduce(jnp.maximum, picked), axis=0, keepdims=True))
        slabs = [jnp.where(h, -jnp.inf, sl) for h, sl in zip(hits, slabs)]
    return jnp.concatenate(vals, axis=0), jnp.concatenate(ids, axis=0)


def _route_kernel(hyb_ref, p1_ref, sgb_ref, x1_ref, x2_ref, who_ref, wo_ref, g2_ref, wq_ref, keys_ref,
                  h_ref, hn_ref, idx_ref, gate_ref, q_scr, idx_scr, gate_scr, cand_scr, cidx_scr,
                  *, idx_scale, n_first):
    yb = _dot(hyb_ref[...], who_ref[...])
    mixed = p1_ref[...] + sgb_ref[...] * yb
    h = _two_part_block(x1_ref, x2_ref, n_first) + _dot(mixed.astype(BF16), wo_ref[...])
    hn = _rms(h, g2_ref[...])
    for c in range(h.shape[1] // LANES):
        h_ref[:, c, :] = h[:, c * LANES:(c + 1) * LANES]
        hn_ref[:, c, :] = hn[:, c * LANES:(c + 1) * LANES]
    q = _dot(hn.astype(BF16), wq_ref[...])
    n_half = 2 * PEER_HEADS
    dh = q.shape[1] // n_half
    for j in range(n_half):
        q_scr[j] = q[:, j * dh:(j + 1) * dh].astype(BF16)

    def head(hd, carry):
        sv0, si0 = _top16(_dot_nt(keys_ref[2 * hd], q_scr[2 * hd]))
        sv1, si1 = _top16(_dot_nt(keys_ref[2 * hd + 1], q_scr[2 * hd + 1]))
        cand_scr[...] = jnp.full(cand_scr.shape, -jnp.inf, F32)
        cidx_scr[...] = jnp.full(cidx_scr.shape, -1.0, F32)
        off = 0
        for a in range(PEER_TOPK):
            nb = PEER_TOPK // (a + 1)
            cand_scr[off:off + nb, :] = sv0[a:a + 1] + sv1[:nb]
            cidx_scr[off:off + nb, :] = (si0[a:a + 1] * float(PEER_NKEYS) + si1[:nb]) * float(idx_scale)
            off += nb
        best, eid = _top16(cand_scr[...], cidx_scr[...])
        ex = jnp.exp(best - best[0:1])
        r0 = pl.multiple_of(hd * PEER_TOPK, PEER_TOPK)
        idx_scr[pl.ds(r0, PEER_TOPK), :] = eid.astype(I32)
        gate_scr[pl.ds(r0, PEER_TOPK), :] = ex / jnp.sum(ex, axis=0, keepdims=True)
        return carry

    lax.fori_loop(0, PEER_HEADS, head, 0)
    slot = lax.broadcasted_iota(I32, idx_ref.shape, 1)
    idx_ref[...] = idx_scr[...].T + jnp.where(slot % SUBLANES < SUBLANES // 2, SUBLANES // 2, 0)
    gate_ref[...] = gate_scr[...].T


def _route(hyb, p1, sg, x1, x2, who_bf, wo_bf, g2, wq_bf, keys_bf, tm, idx_scale):
    d = x1.shape[1]
    t = x1.shape[0] + x2.shape[0]
    n_first = x1.shape[0] // tm
    d_h = hyb.shape[1]
    n_half, nk, dh = keys_bf.shape
    n_cand = sum(PEER_TOPK // (a + 1) for a in range(PEER_TOPK))
    n_cand = -(-n_cand // SUBLANES) * SUBLANES
    tok = lambda w: pl.BlockSpec((tm, w), lambda i: (i, 0))
    full = lambda a: pl.BlockSpec(a.shape, lambda i: tuple(0 for _ in a.shape))
    tok3 = pl.BlockSpec((tm, d // LANES, LANES), lambda i: (i, 0, 0))
    return pl.pallas_call(
        functools.partial(_route_kernel, idx_scale=idx_scale, n_first=n_first),
        grid=(t // tm,),
        in_specs=[tok(d_h), tok(d), pl.BlockSpec((tm, d), lambda i: (i, 1))] + _two_part_specs(tm, d, n_first) + [
                  full(who_bf), full(wo_bf), pl.BlockSpec((1, d), lambda i: (0, 0)), full(wq_bf), full(keys_bf)],
        out_specs=[tok3, tok3, tok(PEER_SLOTS), tok(PEER_SLOTS)],
        out_shape=[jax.ShapeDtypeStruct((t, d // LANES, LANES), F32), jax.ShapeDtypeStruct((t, d // LANES, LANES), F32),
                   jax.ShapeDtypeStruct((t, PEER_SLOTS), I32), jax.ShapeDtypeStruct((t, PEER_SLOTS), F32)],
        scratch_shapes=[pltpu.VMEM((n_half, tm, dh), BF16),
                        pltpu.VMEM((PEER_SLOTS, tm), I32),
                        pltpu.VMEM((PEER_SLOTS, tm), F32),
                        pltpu.VMEM((n_cand, tm), F32),
                        pltpu.VMEM((n_cand, tm), F32)],
        compiler_params=_params(("parallel",), 56),
        name="route",
    )(hyb, p1, sg, x1, x2, who_bf, wo_bf, g2.reshape(1, d), wq_bf, keys_bf)


def _slot_pair(tab_ref, ids, j, e_a, e_b):
    rows = SUBLANES // 2
    assert tab_ref.shape[0] == PEER_NKEYS * PEER_NKEYS * rows + SUBLANES
    assert e_a % SUBLANES < rows <= e_b % SUBLANES
    lo = tab_ref[pl.ds(pl.multiple_of(ids[j, e_a], rows), SUBLANES), :]
    hi = tab_ref[pl.ds(pl.multiple_of(ids[j, e_b], rows), SUBLANES), :]
    upper = lax.broadcasted_iota(I32, (SUBLANES, LANES), 0) >= rows
    return jnp.where(upper, hi, lo)


def _v_position(e):
    return (e // SUBLANES) * SUBLANES + 2 * (e % 4) + (e % SUBLANES) // 4


def _split2(x):
    hi = x.astype(BF16)
    return hi, (x - hi.astype(F32)).astype(BF16)


def _gelu(x):
    return 0.5 * x * (1.0 + jnp.tanh(math.sqrt(2.0 / math.pi) * (x + 0.044715 * (x * x * x))))


ID_CHUNK = 128
V_SLOTS_PER_DOT = 32


def _id_copy(idx_hbm, chunk, dst, sem):
    return pltpu.make_async_copy(idx_hbm.at[pl.ds(chunk * ID_CHUNK, ID_CHUNK), :], dst, sem)


def _token_pipeline(idx_hbm, id_bufs, sems, tb, region):
    chunks_per_step = tb // ID_CHUNK
    assert chunks_per_step % 2 == 0
    first = pl.program_id(0) * chunks_per_step
    total = pl.num_programs(0) * chunks_per_step

    def two_chunks(k, carry):
        c0 = first + 2 * k

        @pl.when(c0 == 0)
        def _():
            _id_copy(idx_hbm, c0, id_bufs[0], sems.at[0]).start()

        _id_copy(idx_hbm, c0 + 1, id_bufs[1], sems.at[1]).start()
        _id_copy(idx_hbm, c0, id_bufs[0], sems.at[0]).wait()
        region(id_bufs[0], 2 * k * ID_CHUNK)

        @pl.when(c0 + 2 < total)
        def _():
            _id_copy(idx_hbm, c0 + 2, id_bufs[0], sems.at[0]).start()

        _id_copy(idx_hbm, c0 + 1, id_bufs[1], sems.at[1]).wait()
        region(id_bufs[1], (2 * k + 1) * ID_CHUNK)
        return carry

    lax.fori_loop(0, chunks_per_step // 2, two_chunks, 0)


def _id_scratch():
    return [pltpu.SMEM((ID_CHUNK, PEER_SLOTS), I32), pltpu.SMEM((ID_CHUNK, PEER_SLOTS), I32),
            pltpu.SemaphoreType.DMA((2,))]


U_GROUP_ORDER = (0, 2, 1, 3)


def _peer_u_kernel(idx_hbm, x_ref, gate_ref, rep_ref, tab_ref, arep_ref, pre_ref, id_a, id_b, sems):
    tb = x_ref.shape[0]
    rows = (tab_ref.shape[0] - SUBLANES) // (PEER_NKEYS * PEER_NKEYS)
    assert rows == 4, "the sublane butterfly below is written for 4 packed rows per expert"
    ones = jnp.ones((SUBLANES, LANES), BF16)
    sub = lax.broadcasted_iota(I32, (SUBLANES, LANES), 0)
    low_pair = (sub % 4) < 2
    even_row = (sub % 2) == 0

    def compute(t, ids, jt):
        x_lo = jnp.concatenate([x_ref[t, pl.ds(0, rows, stride=2), :]] * 2, axis=0)
        x_hi = jnp.concatenate([x_ref[t, pl.ds(1, rows, stride=2), :]] * 2, axis=0)
        merged = []
        for m in range(PEER_SLOTS // SUBLANES):
            a = []
            for j in range(4):
                e = SUBLANES * m + U_GROUP_ORDER[j]
                w = _slot_pair(tab_ref, ids, jt, e, e + 4)
                p = pltpu.bitcast(w << 16, F32) * x_lo + pltpu.bitcast(w & HI_MASK, F32) * x_hi
                a.append(p + pltpu.roll(p, SUBLANES - 2, axis=0))
            ab = jnp.where(low_pair, a[0], pltpu.roll(a[1], 2, axis=0))
            cd = jnp.where(low_pair, a[2], pltpu.roll(a[3], 2, axis=0))
            ab = ab + pltpu.roll(ab, SUBLANES - 1, axis=0)
            cd = cd + pltpu.roll(cd, SUBLANES - 1, axis=0)
            merged.append(jnp.where(even_row, ab, pltpu.roll(cd, 1, axis=0)))
        p_hi, p_lo = _split2(jnp.concatenate(merged, axis=0))
        sums = _dot_nt(ones, p_hi) + _dot_nt(ones, p_lo)
        pre_ref[pl.ds(t, 1), :] = sums[0:1]

    def region(ids, t_base):
        for j in range(ID_CHUNK):
            compute(t_base + j, ids, j)

    _token_pipeline(idx_hbm, (id_a, id_b), sems, tb, region)
    act = _gelu(pre_ref[...]) * gate_ref[...]
    a1 = act.astype(BF16)
    r1 = act - a1.astype(F32)
    a2, a3 = _split2(r1)
    rep = rep_ref[...]
    arep_ref[...] = _dot(a1, rep) + _dot(a2, rep) + _dot(a3, rep)


def _peer_u(idx, x3, gate, tab, tb):
    t, nch, _ = x3.shape
    rep = np.zeros((PEER_SLOTS, PEER_SLOTS * nch), np.float32)
    for e in range(PEER_SLOTS):
        rep[e, _v_position(e) * nch:(_v_position(e) + 1) * nch] = 1.0
    return pl.pallas_call(
        _peer_u_kernel,
        grid=(t // tb,),
        in_specs=[pl.BlockSpec(memory_space=pl.ANY),
                  pl.BlockSpec((tb, nch, LANES), lambda i: (i, 0, 0)),
                  pl.BlockSpec((tb, PEER_SLOTS), lambda i: (i, 0)),
                  pl.BlockSpec(rep.shape, lambda i: (0, 0)),
                  pl.BlockSpec(tab.shape, lambda i: (0, 0), pipeline_mode=pl.Buffered(1))],
        out_specs=pl.BlockSpec((tb, PEER_SLOTS * nch), lambda i: (i, 0)),
        out_shape=jax.ShapeDtypeStruct((t, PEER_SLOTS * nch), F32),
        scratch_shapes=[pltpu.VMEM((tb, PEER_SLOTS), F32)] + _id_scratch(),
        compiler_params=_params(("arbitrary",), 48),
        name="peer_u",
    )(idx, x3, gate, jnp.asarray(rep, BF16), tab)


def _peer_v_kernel(idx_hbm, arep_ref, h_ref, g_ref, tab_ref, o1_ref, o2_ref, y_ref, id_a, id_b, sems, *, n_first):
    tb, nch, _ = h_ref.shape
    shape = (nch, arep_ref.shape[1])
    own_chunk = (lax.broadcasted_iota(I32, shape, 1) % nch) == lax.broadcasted_iota(I32, shape, 0)

    def compute(t, ids, j):
        w_hi, w_lo = _split2(jnp.where(own_chunk, arep_ref[pl.ds(t, 1), :], 0.0))
        acc = None
        for e0 in range(0, PEER_SLOTS, V_SLOTS_PER_DOT):
            pieces = [_slot_pair(tab_ref, ids, j, e8 + r, e8 + r + 4)
                      for e8 in range(e0, e0 + V_SLOTS_PER_DOT, SUBLANES) for r in range(4)]
            vals = pltpu.bitcast(jnp.concatenate(pieces, axis=0), BF16)
            k0, k1 = e0 * nch, (e0 + V_SLOTS_PER_DOT) * nch
            part = _dot(w_hi[:, k0:k1], vals) + _dot(w_lo[:, k0:k1], vals)
            acc = part if acc is None else acc + part
        y_ref[t] = acc

    def region(ids, t_base):
        for j in range(ID_CHUNK):
            compute(t_base + j, ids, j)

    _token_pipeline(idx_hbm, (id_a, id_b), sems, tb, region)
    y = h_ref[...] + y_ref[...]
    ms = jnp.sum(jnp.sum(y * y, axis=2, keepdims=True), axis=1, keepdims=True) * (1.0 / (y.shape[1] * y.shape[2]))
    y_ref[...] = y * lax.rsqrt(ms + EPS) * g_ref[...]

    def emit(o_ref):
        for c in range(nch):
            o_ref[:, c * LANES:(c + 1) * LANES] = y_ref[:, c, :]

    @pl.when(pl.program_id(0) < n_first)
    def _():
        emit(o1_ref)

    @pl.when(pl.program_id(0) >= n_first)
    def _():
        emit(o2_ref)


def _peer_v(idx, arep, h3, g, tab, tb, t_first):
    t, nch, _ = h3.shape
    d = nch * LANES
    n_first = t_first // tb
    blk = pl.BlockSpec((tb, nch, LANES), lambda i: (i, 0, 0))
    return pl.pallas_call(
        functools.partial(_peer_v_kernel, n_first=n_first),
        grid=(t // tb,),
        in_specs=[pl.BlockSpec(memory_space=pl.ANY),
                  pl.BlockSpec((tb, arep.shape[1]), lambda i: (i, 0)),
                  blk,
                  pl.BlockSpec((1, nch, LANES), lambda i: (0, 0, 0)),
                  pl.BlockSpec(tab.shape, lambda i: (0, 0), pipeline_mode=pl.Buffered(1))],
        out_specs=[pl.BlockSpec((tb, d), lambda i: (jnp.minimum(i, n_first - 1), 0)),
                   pl.BlockSpec((tb, d), lambda i: (jnp.maximum(i - n_first, 0), 0))],
        out_shape=[jax.ShapeDtypeStruct((t_first, d), F32), jax.ShapeDtypeStruct((t - t_first, d), F32)],
        scratch_shapes=[pltpu.VMEM((tb, nch, LANES), F32)] + _id_scratch(),
        compiler_params=_params(("arbitrary",), 48),
        name="peer_v",
    )(idx, arep, h3, g.reshape(1, nch, LANES), tab)


def _pick_tile(n, want):
    while n % want:
        want //= 2
    return want


def kernel(x_prompt, x_sample, norm1_g, w_in, pool_w, pool_scale, conv_w, conv_b, filt_w1, filt_b1, filt_f1, filt_w2, filt_b2, filt_f2, filt_w3, hyena_bias, w_pool_out, w_hyena_out, w_o, norm2_g, peer_wq, peer_keys, peer_u, peer_v, normf_g):
    assert norm1_g.shape[0] == 1, "single-layer block"
    assert x_prompt.shape[1:] == x_sample.shape[1:]
    nb_p, seq_len, d = x_prompt.shape
    d_pool = pool_scale.shape[-1]
    d_hy = conv_b.shape[-1]
    d_h = d_hy // 3
    x1 = x_prompt.reshape(-1, d)
    x2 = x_sample.reshape(-1, d)
    tm = _pick_tile(seq_len, 512)

    ahy, sg = _inproj(x1, x2, norm1_g[0], w_in[0].astype(BF16), d_pool + d_hy, tm)
    p1, x0, z, zb = _local(ahy, sg, pool_w[0].astype(BF16), pool_scale[0], conv_w[0], conv_b[0],
                           w_pool_out[0].astype(BF16), seq_len, d_pool, d_hy, tm)
    ks, kd = _filters(filt_w1[0], filt_b1[0], filt_f1[0], filt_w2[0], filt_b2[0], filt_f2[0], filt_w3[0],
                      seq_len, d_h)
    cmat, smat, stmat = _dftgen(seq_len)
    kr, ki = _spectrum(cmat, smat, ks, kd, tm)
    hyb = _long_conv(cmat, smat, stmat, zb, z, x0, kr, ki, hyena_bias[0], seq_len, tm)

    keys = peer_keys[0]
    keys_bf = keys.reshape((-1,) + keys.shape[2:]).astype(BF16)
    h, hn, idx, gate = _route(hyb, p1, sg, x1, x2, w_hyena_out[0].astype(BF16), w_o[0].astype(BF16), norm2_g[0],
                              peer_wq[0].astype(BF16), keys_bf, _pick_tile(seq_len, 512),
                              idx_scale=d // (2 * LANES))

    tb = _pick_tile(seq_len, 256)
    arep = _peer_u(idx, hn, gate, _pack_table(peer_u[0]), tb)
    y_p, y_s = _peer_v(idx, arep, h, normf_g, _pack_table(peer_v[0]), tb,
                       nb_p * seq_len)
    return (y_p.reshape(x_prompt.shape), y_s.reshape(x_sample.shape))
```
